```python
import math
import jax, jax.numpy as jnp
from jax import lax
import numpy as np

D_MODEL = 1024
BATCH = 8
SEQ = 4096
DEPTH = 2

N_BRANCH = 4
BR_WIDTH = D_MODEL // N_BRANCH
HEAD_DIM = 64
N_HEADS = BR_WIDTH // HEAD_DIM
CHUNK = 128
Q_BLOCK = 128
SHORT_CONV = 3
CONF_CONV = 31
EPS = 1e-6

SPLIT_SIZES = (
    2 * BR_WIDTH,
    BR_WIDTH,
    3 * BR_WIDTH,
    N_HEADS,
    BR_WIDTH,
    3 * BR_WIDTH,
    BR_WIDTH,
    2 * BR_WIDTH,
    BR_WIDTH,
    N_BRANCH * D_MODEL,
)
IN_COLS = sum(SPLIT_SIZES)
SPLIT_POINTS = tuple(int(v) for v in np.cumsum(SPLIT_SIZES)[:-1])

kernel_name = "hybrid_gated_parallel_mixers"


def rmsnorm(x, g):
    x32 = x.astype(jnp.float32)
    y = x32 * lax.rsqrt(jnp.mean(x32 * x32, axis=-1, keepdims=True) + EPS)
    return y.astype(x.dtype) * g


def layernorm(x, g, b):
    x32 = x.astype(jnp.float32)
    mu = jnp.mean(x32, axis=-1, keepdims=True)
    xc = x32 - mu
    var = jnp.mean(xc * xc, axis=-1, keepdims=True)
    return (xc * lax.rsqrt(var + EPS)).astype(x.dtype) * g + b


def causal_depthwise_conv(x, w):
    k_width, ch = w.shape
    return lax.conv_general_dilated(
        x, w.astype(x.dtype)[:, None, :],
        window_strides=(1,), padding=[(k_width - 1, 0)],
        dimension_numbers=("NWC", "WIO", "NWC"),
        feature_group_count=ch)


def gmlp_chunk_mixer(uv, sgu_w, sgu_b, ln_g, ln_b):
    bsz, seq, _ = uv.shape
    u, v = jnp.split(jax.nn.gelu(uv), 2, axis=-1)
    v = layernorm(v, ln_g, ln_b)
    n_chunks = seq // CHUNK
    vc = v.reshape(bsz, n_chunks, CHUNK, N_HEADS, HEAD_DIM)
    causal = jnp.tril(jnp.ones((CHUNK, CHUNK), dtype=bool))
    w = jnp.where(causal[None], sgu_w, jnp.zeros_like(sgu_w))
    mixed = jnp.einsum("hts,bnshd->bnthd", w, vc) + sgu_b.T[None, None, :, :, None]
    return u * mixed.reshape(bsz, seq, BR_WIDTH)


def forgetting_attention(qkv, f_logit, f_bias):
    bsz, seq, _ = qkv.shape
    q, k, v = jnp.split(qkv, 3, axis=-1)
    q = q.reshape(bsz, seq, N_HEADS, HEAD_DIM)
    k = k.reshape(bsz, seq, N_HEADS, HEAD_DIM)
    v = v.reshape(bsz, seq, N_HEADS, HEAD_DIM)
    log_f = jax.nn.log_sigmoid(f_logit.astype(jnp.float32) + f_bias.astype(jnp.float32))
    cum = jnp.cumsum(log_f, axis=1).transpose(0, 2, 1)
    n_blocks = seq // Q_BLOCK
    q_blocks = q.reshape(bsz, n_blocks, Q_BLOCK, N_HEADS, HEAD_DIM).transpose(1, 0, 2, 3, 4)
    c_blocks = cum.reshape(bsz, N_HEADS, n_blocks, Q_BLOCK).transpose(2, 0, 1, 3)
    key_pos = jnp.arange(seq)
    scale = 1.0 / math.sqrt(HEAD_DIM)
    neg = jnp.finfo(jnp.float32).min

    def one_block(args):
        qb, cb, idx = args
        s = jnp.einsum("bqhd,bkhd->bhqk", qb, k).astype(jnp.float32) * scale
        s = s + (cb[..., :, None] - cum[:, :, None, :])
        q_pos = idx * Q_BLOCK + jnp.arange(Q_BLOCK)
        mask = key_pos[None, :] <= q_pos[:, None]
        p = jax.nn.softmax(jnp.where(mask, s, neg), axis=-1).astype(v.dtype)
        return jnp.einsum("bhqk,bkhd->bqhd", p, v)

    out = lax.map(one_block, (q_blocks, c_blocks, jnp.arange(n_blocks)))
    return out.transpose(1, 0, 2, 3, 4).reshape(bsz, seq, BR_WIDTH)


def short_gated_conv(cin, conv_w):
    b_gate, c_gate, xin = jnp.split(cin, 3, axis=-1)
    return b_gate * causal_depthwise_conv(c_gate * xin, conv_w)


def conformer_conv(glu_in, dw_w, dw_b, ln_g, ln_b):
    a, g = jnp.split(glu_in, 2, axis=-1)
    h = a * jax.nn.sigmoid(g)
    h = causal_depthwise_conv(h, dw_w) + dw_b
    return jax.nn.silu(layernorm(h, ln_g, ln_b))


def setup_inputs(seed: int = 0) -> dict:
    key = jax.random.key(seed)
    ks = jax.random.split(key, 18)
    f32 = jnp.float32
    nrm = lambda k, shape, s: jax.random.normal(k, shape, f32) * s
    return {
        "x": jax.random.normal(ks[0], (BATCH, SEQ, D_MODEL), f32),
        "norm_g": 1.0 + nrm(ks[1], (DEPTH, D_MODEL), 0.02),
        "w_in": nrm(ks[2], (DEPTH, D_MODEL, IN_COLS), D_MODEL ** -0.5),
        "f_bias": jax.random.uniform(ks[3], (DEPTH, N_HEADS), f32, 1.0, 4.0),
        "sgu_w": nrm(ks[4], (DEPTH, N_HEADS, CHUNK, CHUNK), CHUNK ** -0.5),
        "sgu_b": 1.0 + nrm(ks[5], (DEPTH, N_HEADS, CHUNK), 0.02),
        "sgu_ln_g": 1.0 + nrm(ks[6], (DEPTH, BR_WIDTH), 0.02),
        "sgu_ln_b": nrm(ks[7], (DEPTH, BR_WIDTH), 0.02),
        "short_conv_w": nrm(ks[8], (DEPTH, SHORT_CONV, BR_WIDTH), SHORT_CONV ** -0.5),
        "conf_dw_w": nrm(ks[9], (DEPTH, CONF_CONV, BR_WIDTH), CONF_CONV ** -0.5),
        "conf_dw_b": nrm(ks[10], (DEPTH, BR_WIDTH), 0.02),
        "conf_ln_g": 1.0 + nrm(ks[11], (DEPTH, BR_WIDTH), 0.02),
        "conf_ln_b": nrm(ks[12], (DEPTH, BR_WIDTH), 0.02),
        "w_branch": nrm(ks[13], (DEPTH, N_BRANCH, BR_WIDTH, D_MODEL), BR_WIDTH ** -0.5),
        "w_out": nrm(ks[14], (DEPTH, D_MODEL, D_MODEL), D_MODEL ** -0.5),
        "final_g": 1.0 + nrm(ks[15], (D_MODEL,), 0.02),
    }


def reference(x, norm_g, w_in, f_bias, sgu_w, sgu_b, sgu_ln_g, sgu_ln_b,
              short_conv_w, conf_dw_w, conf_dw_b, conf_ln_g, conf_ln_b,
              w_branch, w_out, final_g):
    bsz, seq, _ = x.shape
    for layer in range(DEPTH):
        h = rmsnorm(x, norm_g[layer])
        proj = jnp.einsum("bsd,dc->bsc", h, w_in[layer])
        (a_uv, a_gate, b_qkv, b_f, b_gate, c_in, c_gate,
         d_glu, d_gate, merge_logits) = jnp.split(proj, SPLIT_POINTS, axis=-1)

        y_a = gmlp_chunk_mixer(a_uv, sgu_w[layer], sgu_b[layer],
                               sgu_ln_g[layer], sgu_ln_b[layer]) * jax.nn.silu(a_gate)
        y_b = forgetting_attention(b_qkv, b_f, f_bias[layer]) * jax.nn.silu(b_gate)
        y_c = short_gated_conv(c_in, short_conv_w[layer]) * jax.nn.silu(c_gate)
        y_d = conformer_conv(d_glu, conf_dw_w[layer], conf_dw_b[layer],
                             conf_ln_g[layer], conf_ln_b[layer]) * jax.nn.silu(d_gate)

        branches = jnp.stack([y_a, y_b, y_c, y_d], axis=2)
        projected = jnp.einsum("bsnc,ncd->bsnd", branches, w_branch[layer])
        gates = jax.nn.sigmoid(merge_logits.reshape(bsz, seq, N_BRANCH, D_MODEL))
        merged = jnp.sum(gates * projected, axis=2)
        x = x + jnp.einsum("bsd,de->bse", merged, w_out[layer])
    return rmsnorm(x, final_g)
```

```python
import functools
import math

import jax
import jax.numpy as jnp
from jax import lax
from jax.experimental import pallas as pl
from jax.experimental.pallas import tpu as pltpu

D_MODEL = 1024
N_BRANCH = 4
BR = D_MODEL // N_BRANCH
HEAD_DIM = 64
N_HEADS = BR // HEAD_DIM
CHUNK = 128
SHORT_CONV = 3
CONF_CONV = 31
EPS = 1e-6

LANES = 128
SUBLANES = 8
AUG = LANES
CONF_HALO = 32
SHORT_HALO = SUBLANES

T_MIX = 512
T_ATT = 256
T_OUT = 512
VMEM_LIMIT = 56 * 1024 * 1024

C_UV, C_AG, C_Q, C_K, C_F, C_BG, C_C, C_CG, C_D, C_DG, C_END = (
    0, 512, 768, 1280, 1792, 1920, 2176, 2944, 3200, 3712, 3968)

NEG = -1e30

_NT = (((1,), (1,)), ((), ()))


def _dot(a, b):
    return jnp.dot(a, b, preferred_element_type=jnp.float32)


def _sigmoid(x):
    return 1.0 / (1.0 + jnp.exp(-x))


def _silu(x):
    return x * _sigmoid(x)


def _gelu_tanh(x):
    c = math.sqrt(2.0 / math.pi)
    return 0.5 * x * (1.0 + jnp.tanh(c * (x + 0.044715 * (x * x * x))))


def _log_sigmoid(x):
    return jnp.minimum(x, 0.0) - jnp.log(1.0 + jnp.exp(-jnp.abs(x)))


def _layernorm(x, g, b):
    mu = jnp.mean(x, axis=-1, keepdims=True)
    xc = x - mu
    var = jnp.mean(xc * xc, axis=-1, keepdims=True)
    return xc * lax.rsqrt(var + EPS) * g + b


def _rms_scale(x):
    return x * lax.rsqrt(jnp.mean(x * x, axis=-1, keepdims=True) + EPS)


def _split3(x):
    hi = x.astype(jnp.bfloat16).astype(jnp.float32)
    r = x - hi
    mid = r.astype(jnp.bfloat16).astype(jnp.float32)
    lo = r - mid
    return hi, mid, lo


def _mixer_in_kernel(x_ref, g_ref, w_ref, wvt_ref, fb_ref, sguw_ref, sgub_ref,
                     aln_ref, scw_ref, dww_ref, dvec_ref,
                     ya_ref, yc_ref, yd_ref, gb_ref, q_ref, k_ref, vt_ref,
                     zbuf, hbuf, cum_carry):
    t = x_ref.shape[1]
    j = pl.program_id(1)

    @pl.when(j == 0)
    def _():
        zbuf[0:SHORT_HALO, :] = jnp.zeros((SHORT_HALO, BR), jnp.float32)
        hbuf[0:CONF_HALO, :] = jnp.zeros((CONF_HALO, BR), jnp.float32)
        cum_carry[...] = jnp.zeros_like(cum_carry)

    x = x_ref[0]
    hb = (_rms_scale(x) * g_ref[...]).astype(jnp.bfloat16)

    gel = _gelu_tanh(_dot(hb, w_ref[:, C_UV:C_AG]))
    u = gel[:, :BR]
    v = _layernorm(gel[:, BR:], aln_ref[0:1, :], aln_ref[1:2, :])
    ga = _silu(_dot(hb, w_ref[:, C_AG:C_Q]))
    rows = lax.broadcasted_iota(jnp.int32, (N_HEADS * CHUNK, CHUNK), 0)
    cols = lax.broadcasted_iota(jnp.int32, (N_HEADS * CHUNK, CHUNK), 1)
    wst = jnp.where((rows % CHUNK) >= cols, sguw_ref[...], 0.0).astype(jnp.bfloat16)
    lane_head = lax.broadcasted_iota(jnp.int32, (CHUNK, BR), 1) // HEAD_DIM
    for c in range(t // CHUNK):
        sl = slice(c * CHUNK, (c + 1) * CHUNK)
        mall = _dot(wst, v[sl].astype(jnp.bfloat16))
        mixed = sgub_ref[...]
        for h in range(N_HEADS):
            mixed = mixed + jnp.where(lane_head == h, mall[h * CHUNK:(h + 1) * CHUNK], 0.0)
        ya_ref[0, sl, :] = (u[sl] * mixed * ga[sl]).astype(ya_ref.dtype)

    cin = _dot(hb, w_ref[:, C_C:C_CG])
    zbuf[SHORT_HALO:SHORT_HALO + t, :] = cin[:, BR:2 * BR] * cin[:, 2 * BR:]
    conv = jnp.zeros((t, BR), jnp.float32)
    for kk in range(SHORT_CONV):
        s = SHORT_CONV - 1 - kk
        conv = conv + zbuf[SHORT_HALO - s:SHORT_HALO - s + t, :] * scw_ref[kk:kk + 1, :]
    gc = _silu(_dot(hb, w_ref[:, C_CG:C_D]))
    yc_ref[0] = (cin[:, :BR] * conv * gc).astype(yc_ref.dtype)
    zbuf[0:SHORT_HALO, :] = zbuf[t:t + SHORT_HALO, :]

    glu = _dot(hb, w_ref[:, C_D:C_DG])
    hbuf[CONF_HALO:CONF_HALO + t, :] = glu[:, :BR] * _sigmoid(glu[:, BR:])
    dconv = jnp.zeros((t, BR), jnp.float32) + dvec_ref[0:1, :]
    for kk in range(CONF_CONV):
        s = CONF_CONV - 1 - kk
        dconv = dconv + hbuf[CONF_HALO - s:CONF_HALO - s + t, :] * dww_ref[kk:kk + 1, :]
    gd = _silu(_dot(hb, w_ref[:, C_DG:C_END]))
    yd_ref[0] = (_silu(_layernorm(dconv, dvec_ref[1:2, :], dvec_ref[2:3, :])) * gd
                 ).astype(yd_ref.dtype)
    hbuf[0:CONF_HALO, :] = hbuf[t:t + CONF_HALO, :]

    gb_ref[0] = _silu(_dot(hb, w_ref[:, C_BG:C_C])).astype(gb_ref.dtype)
    vt_ref[0] = lax.dot_general(wvt_ref[...], hb, _NT,
                                preferred_element_type=jnp.float32).astype(vt_ref.dtype)

    lf = _log_sigmoid(_dot(hb, w_ref[:, C_F:C_BG]) + fb_ref[...])
    r2 = lax.broadcasted_iota(jnp.int32, (CHUNK, CHUNK), 0)
    c2 = lax.broadcasted_iota(jnp.int32, (CHUNK, CHUNK), 1)
    tri = jnp.where(r2 >= c2, 1.0, 0.0).astype(jnp.bfloat16)
    carry = cum_carry[...]
    cums = []
    for c in range(t // CHUNK):
        hi, mid, lo = _split3(lf[c * CHUNK:(c + 1) * CHUNK])
        cc = (_dot(tri, hi.astype(jnp.bfloat16)) + _dot(tri, mid.astype(jnp.bfloat16))
              + _dot(tri, lo.astype(jnp.bfloat16))) + carry
        carry = cc[CHUNK - 1:CHUNK, :]
        cums.append(cc)
    cum_carry[...] = carry
    cum = jnp.concatenate(cums, axis=0)

    lane = lax.broadcasted_iota(jnp.int32, (t, AUG), 1)
    ones_q = jnp.where((lane >= HEAD_DIM + 3) & (lane < HEAD_DIM + 6), 1.0, 0.0)
    ones_k = jnp.where((lane >= HEAD_DIM) & (lane < HEAD_DIM + 3), 1.0, 0.0)
    for h in range(N_HEADS):
        hi, mid, lo = _split3(jnp.broadcast_to(cum[:, h:h + 1], (t, AUG)))
        eq = jnp.where(lane == HEAD_DIM, hi,
                       jnp.where(lane == HEAD_DIM + 1, mid,
                                 jnp.where(lane == HEAD_DIM + 2, lo, ones_q)))
        ek = jnp.where(lane == HEAD_DIM + 3, -hi,
                       jnp.where(lane == HEAD_DIM + 4, -mid,
                                 jnp.where(lane == HEAD_DIM + 5, -lo, ones_k)))
        qh = _dot(hb, w_ref[:, C_Q + h * AUG:C_Q + (h + 1) * AUG])
        kh = _dot(hb, w_ref[:, C_K + h * AUG:C_K + (h + 1) * AUG])
        q_ref[0, h] = (qh + eq).astype(q_ref.dtype)
        k_ref[0, h] = (kh + ek).astype(k_ref.dtype)


def _mixer_in(x, g, w1, wvt, fb, sguw, sgub, aln, scw, dww, dvec):
    bsz, seq, d = x.shape
    t = T_MIX
    nblk = seq // t
    bf = jnp.bfloat16
    const = lambda shape: pl.BlockSpec(shape, lambda b, j: (0,) * len(shape))
    tok = lambda width: pl.BlockSpec((1, t, width), lambda b, j: (b, j, 0))
    return pl.pallas_call(
        _mixer_in_kernel,
        grid=(bsz, nblk),
        in_specs=[
            tok(d), const(g.shape), const(w1.shape), const(wvt.shape), const(fb.shape),
            const(sguw.shape), const(sgub.shape), const(aln.shape), const(scw.shape),
            const(dww.shape), const(dvec.shape),
        ],
        out_specs=[
            tok(BR), tok(BR), tok(BR), tok(BR),
            pl.BlockSpec((1, N_HEADS, t, AUG), lambda b, j: (b, 0, j, 0)),
            pl.BlockSpec((1, N_HEADS, t, AUG), lambda b, j: (b, 0, j, 0)),
            pl.BlockSpec((1, BR, t), lambda b, j: (b, 0, j)),
        ],
        out_shape=[
            jax.ShapeDtypeStruct((bsz, seq, BR), bf),
            jax.ShapeDtypeStruct((bsz, seq, BR), bf),
            jax.ShapeDtypeStruct((bsz, seq, BR), bf),
            jax.ShapeDtypeStruct((bsz, seq, BR), bf),
            jax.ShapeDtypeStruct((bsz, N_HEADS, seq, AUG), bf),
            jax.ShapeDtypeStruct((bsz, N_HEADS, seq, AUG), bf),
            jax.ShapeDtypeStruct((bsz, BR, seq), bf),
        ],
        scratch_shapes=[
            pltpu.VMEM((SHORT_HALO + t, BR), jnp.float32),
            pltpu.VMEM((CONF_HALO + t, BR), jnp.float32),
            pltpu.VMEM((1, LANES), jnp.float32),
        ],
        compiler_params=pltpu.CompilerParams(
            dimension_semantics=("arbitrary", "arbitrary"), vmem_limit_bytes=VMEM_LIMIT),
        name="mixer_in",
    )(x, g, w1, wvt, fb, sguw, sgub, aln, scw, dww, dvec)


def _fox_attn_kernel(q_ref, k_ref, vt_ref, o_ref):
    tq = q_ref.shape[2]
    qi = pl.program_id(2)
    q = q_ref[0, 0]

    def tile(kt, carry, masked):
        m, l, acc = carry
        ks = pl.multiple_of(kt * tq, tq)
        k = k_ref[0, 0, pl.ds(ks, tq), :]
        s = lax.dot_general(k, q, _NT, preferred_element_type=jnp.float32)
        if masked:
            kpos = lax.broadcasted_iota(jnp.int32, s.shape, 0)
            qpos = lax.broadcasted_iota(jnp.int32, s.shape, 1)
            s = jnp.where(kpos <= qpos, s, NEG)
        m_new = jnp.maximum(m, jnp.max(s, axis=0, keepdims=True))
        alpha = jnp.exp(m - m_new)
        p = jnp.exp(s - m_new)
        l = alpha * l + jnp.sum(p, axis=0, keepdims=True)
        vt = vt_ref[0, :, pl.ds(ks, tq)]
        acc = alpha * acc + _dot(vt, p.astype(jnp.bfloat16))
        return m_new, l, acc

    init = (jnp.full((1, tq), NEG, jnp.float32), jnp.zeros((1, tq), jnp.float32),
            jnp.zeros((HEAD_DIM, tq), jnp.float32))
    carry = lax.fori_loop(0, qi, lambda kt, c: tile(kt, c, False), init)
    m, l, acc = tile(qi, carry, True)
    o_ref[0] = (acc / l).astype(o_ref.dtype)


def _fox_attn(q_aug, k_aug, vt):
    bsz, nh, seq, aug = q_aug.shape
    tq = T_ATT
    return pl.pallas_call(
        _fox_attn_kernel,
        grid=(bsz, nh, seq // tq),
        in_specs=[
            pl.BlockSpec((1, 1, tq, aug), lambda b, h, i: (b, h, i, 0)),
            pl.BlockSpec((1, 1, seq, aug), lambda b, h, i: (b, h, 0, 0)),
            pl.BlockSpec((1, HEAD_DIM, seq), lambda b, h, i: (b, h, 0)),
        ],
        out_specs=pl.BlockSpec((1, HEAD_DIM, tq), lambda b, h, i: (b, h, i)),
        out_shape=jax.ShapeDtypeStruct((bsz, nh * HEAD_DIM, seq), jnp.bfloat16),
        compiler_params=pltpu.CompilerParams(
            dimension_semantics=("arbitrary", "arbitrary", "arbitrary"),
            vmem_limit_bytes=VMEM_LIMIT),
        name="fox_attn",
    )(q_aug, k_aug, vt)


def _merge_out_kernel(x_ref, g_ref, ya_ref, ybt_ref, gb_ref, yc_ref, yd_ref,
                      wm_ref, wb_ref, wo_ref, fg_ref, o_ref, *, final):
    x = x_ref[0]
    hb = (_rms_scale(x) * g_ref[...]).astype(jnp.bfloat16)
    yb = (ybt_ref[0].astype(jnp.float32).T * gb_ref[0].astype(jnp.float32)).astype(jnp.bfloat16)
    ys = (ya_ref[0], yb, yc_ref[0], yd_ref[0])
    merged = jnp.zeros(x.shape, jnp.float32)
    for n in range(N_BRANCH):
        gate = _sigmoid(_dot(hb, wm_ref[:, n * D_MODEL:(n + 1) * D_MODEL]))
        merged = merged + gate * _dot(ys[n], wb_ref[n])
    out = x + _dot(merged.astype(jnp.bfloat16), wo_ref[...])
    if final:
        out = _rms_scale(out) * fg_ref[...]
    o_ref[0] = out


def _merge_out(x, g, ya, ybt, gb, yc, yd, wm, wb, wo, fg, final):
    bsz, seq, d = x.shape
    t = T_OUT
    const = lambda shape: pl.BlockSpec(shape, lambda b, j: (0,) * len(shape))
    tok = lambda width: pl.BlockSpec((1, t, width), lambda b, j: (b, j, 0))
    return pl.pallas_call(
        functools.partial(_merge_out_kernel, final=final),
        grid=(bsz, seq // t),
        in_specs=[
            tok(d), const(g.shape), tok(BR),
            pl.BlockSpec((1, BR, t), lambda b, j: (b, 0, j)),
            tok(BR), tok(BR), tok(BR),
            const(wm.shape), const(wb.shape), const(wo.shape), const(fg.shape),
        ],
        out_specs=tok(d),
        out_shape=jax.ShapeDtypeStruct((bsz, seq, d), jnp.float32),
        compiler_params=pltpu.CompilerParams(
            dimension_semantics=("arbitrary", "arbitrary"), vmem_limit_bytes=VMEM_LIMIT),
        name="merge_out_final" if final else "merge_out",
    )(x, g, ya, ybt, gb, yc, yd, wm, wb, wo, fg)


def _pack_mixer_weight(w_in):
    d = w_in.shape[0]
    a_uv, a_gate, qkv, f, b_gate, c_in, c_gate, d_glu, d_gate = (
        w_in[:, 0:512], w_in[:, 512:768], w_in[:, 768:1536], w_in[:, 1536:1540],
        w_in[:, 1540:1796], w_in[:, 1796:2564], w_in[:, 2564:2820], w_in[:, 2820:3332],
        w_in[:, 3332:3588])
    scale = 1.0 / math.sqrt(HEAD_DIM)
    pad = jnp.zeros((d, AUG - HEAD_DIM), w_in.dtype)
    q_cols, k_cols = [], []
    for h in range(N_HEADS):
        q_cols += [qkv[:, h * HEAD_DIM:(h + 1) * HEAD_DIM] * scale, pad]
        k_cols += [qkv[:, BR + h * HEAD_DIM:BR + (h + 1) * HEAD_DIM], pad]
    f_pad = jnp.concatenate([f, jnp.zeros((d, LANES - N_HEADS), w_in.dtype)], axis=1)
    w1 = jnp.concatenate([a_uv, a_gate] + q_cols + k_cols
                         + [f_pad, b_gate, c_in, c_gate, d_glu, d_gate], axis=1)
    wvt = qkv[:, 2 * BR:].T
    return w1.astype(jnp.bfloat16), wvt.astype(jnp.bfloat16)


def kernel(x, norm_g, w_in, f_bias, sgu_w, sgu_b, sgu_ln_g, sgu_ln_b, short_conv_w,
           conf_dw_w, conf_dw_b, conf_ln_g, conf_ln_b, w_branch, w_out, final_g):
    depth = norm_g.shape[0]
    bf = jnp.bfloat16
    fg = final_g.reshape(1, D_MODEL)
    for layer in range(depth):
        g = norm_g[layer].reshape(1, D_MODEL)
        w1, wvt = _pack_mixer_weight(w_in[layer])
        fb = jnp.zeros((1, LANES), jnp.float32).at[0, :N_HEADS].set(f_bias[layer])
        sguw = sgu_w[layer].reshape(N_HEADS * CHUNK, CHUNK)
        sgub = jnp.repeat(sgu_b[layer].T, HEAD_DIM, axis=1)
        aln = jnp.stack([sgu_ln_g[layer], sgu_ln_b[layer]])
        scw = jnp.zeros((SUBLANES, BR), jnp.float32).at[:SHORT_CONV].set(short_conv_w[layer])
        dww = jnp.zeros((CONF_HALO, BR), jnp.float32).at[:CONF_CONV].set(conf_dw_w[layer])
        dvec = jnp.stack([conf_dw_b[layer], conf_ln_g[layer], conf_ln_b[layer]])
        ya, yc, yd, gb, q_aug, k_aug, vt = _mixer_in(
            x, g, w1, wvt, fb, sguw, sgub, aln, scw, dww, dvec)
        ybt = _fox_attn(q_aug, k_aug, vt)
        wm = w_in[layer][:, 3588:].astype(bf)
        x = _merge_out(x, g, ya, ybt, gb, yc, yd, wm, w_branch[layer].astype(bf),
                       w_out[layer].astype(bf), fg, final=(layer == depth - 1))
    return x
```

```python
import functools
import math

import jax
import jax.numpy as jnp
from jax import lax
from jax.experimental import pallas as pl
from jax.experimental.pallas import tpu as pltpu

D_MODEL = 1024
N_BRANCH = 4
BR = D_MODEL // N_BRANCH
HEAD_DIM = 64
N_HEADS = BR // HEAD_DIM
CHUNK = 128
SHORT_CONV = 3
CONF_CONV = 31
EPS = 1e-6

LANES = 128
SUBLANES = 8
AUG = LANES
CONF_HALO = 32
SHORT_HALO = SUBLANES

T_MIX = 512
T_ATT = 512
V_ROWS = HEAD_DIM + 16
LOG2E = math.log2(math.e)
T_OUT = 512
VMEM_LIMIT = 56 * 1024 * 1024

C_UV, C_AG, C_Q, C_K, C_F, C_BG, C_C, C_CG, C_D, C_DG, C_END = (
    0, 512, 768, 1280, 1792, 1920, 2176, 2944, 3200, 3712, 3968)

NEG = -1e30

_NT = (((1,), (1,)), ((), ()))


def _dot(a, b):
    return jnp.dot(a, b, preferred_element_type=jnp.float32)


def _sigmoid(x):
    return 1.0 / (1.0 + jnp.exp(-x))


def _silu(x):
    return x * _sigmoid(x)


def _gelu_tanh(x):
    c = math.sqrt(2.0 / math.pi)
    return 0.5 * x * (1.0 + jnp.tanh(c * (x + 0.044715 * (x * x * x))))


def _log_sigmoid(x):
    return jnp.minimum(x, 0.0) - jnp.log(1.0 + jnp.exp(-jnp.abs(x)))


def _layernorm(x, g, b):
    mu = jnp.mean(x, axis=-1, keepdims=True)
    xc = x - mu
    var = jnp.mean(xc * xc, axis=-1, keepdims=True)
    return xc * lax.rsqrt(var + EPS) * g + b


def _rms_scale(x):
    return x * lax.rsqrt(jnp.mean(x * x, axis=-1, keepdims=True) + EPS)


def _split3(x):
    hi = x.astype(jnp.bfloat16).astype(jnp.float32)
    r = x - hi
    mid = r.astype(jnp.bfloat16).astype(jnp.float32)
    lo = r - mid
    return hi, mid, lo


def _mixer_in_kernel(x_ref, g_ref, w_ref, wvt_ref, fb_ref, sguw_ref, sgub_ref,
                     aln_ref, scw_ref, dww_ref, dvec_ref,
                     ya_ref, yc_ref, yd_ref, gb_ref, q_ref, k_ref, vt_ref,
                     zbuf, hbuf, cum_carry):
    t = x_ref.shape[1]
    j = pl.program_id(1)

    @pl.when(j == 0)
    def _():
        zbuf[0:SHORT_HALO, :] = jnp.zeros((SHORT_HALO, BR), jnp.float32)
        hbuf[0:CONF_HALO, :] = jnp.zeros((CONF_HALO, BR), jnp.float32)
        cum_carry[...] = jnp.zeros_like(cum_carry)

    x = x_ref[0]
    hb = (_rms_scale(x) * g_ref[...]).astype(jnp.bfloat16)

    gel = _gelu_tanh(_dot(hb, w_ref[:, C_UV:C_AG]))
    u = gel[:, :BR]
    v = _layernorm(gel[:, BR:], aln_ref[0:1, :], aln_ref[1:2, :])
    ga = _silu(_dot(hb, w_ref[:, C_AG:C_Q]))
    rows = lax.broadcasted_iota(jnp.int32, (N_HEADS * CHUNK, CHUNK), 0)
    cols = lax.broadcasted_iota(jnp.int32, (N_HEADS * CHUNK, CHUNK), 1)
    wst = jnp.where((rows % CHUNK) >= cols, sguw_ref[...], 0.0).astype(jnp.bfloat16)
    lane_head = lax.broadcasted_iota(jnp.int32, (CHUNK, BR), 1) // HEAD_DIM
    for c in range(t // CHUNK):
        sl = slice(c * CHUNK, (c + 1) * CHUNK)
        mall = _dot(wst, v[sl].astype(jnp.bfloat16))
        mixed = sgub_ref[...]
        for h in range(N_HEADS):
            mixed = mixed + jnp.where(lane_head == h, mall[h * CHUNK:(h + 1) * CHUNK], 0.0)
        ya_ref[0, sl, :] = (u[sl] * mixed * ga[sl]).astype(ya_ref.dtype)

    cin = _dot(hb, w_ref[:, C_C:C_CG])
    zbuf[SHORT_HALO:SHORT_HALO + t, :] = cin[:, BR:2 * BR] * cin[:, 2 * BR:]
    conv = jnp.zeros((t, BR), jnp.float32)
    for kk in range(SHORT_CONV):
        s = SHORT_CONV - 1 - kk
        conv = conv + zbuf[SHORT_HALO - s:SHORT_HALO - s + t, :] * scw_ref[kk:kk + 1, :]
    gc = _silu(_dot(hb, w_ref[:, C_CG:C_D]))
    yc_ref[0] = (cin[:, :BR] * conv * gc).astype(yc_ref.dtype)
    zbuf[0:SHORT_HALO, :] = zbuf[t:t + SHORT_HALO, :]

    glu = _dot(hb, w_ref[:, C_D:C_DG])
    hbuf[CONF_HALO:CONF_HALO + t, :] = glu[:, :BR] * _sigmoid(glu[:, BR:])
    dconv = jnp.zeros((t, BR), jnp.float32) + dvec_ref[0:1, :]
    for kk in range(CONF_CONV):
        s = CONF_CONV - 1 - kk
        dconv = dconv + hbuf[CONF_HALO - s:CONF_HALO - s + t, :] * dww_ref[kk:kk + 1, :]
    gd = _silu(_dot(hb, w_ref[:, C_DG:C_END]))
    yd_ref[0] = (_silu(_layernorm(dconv, dvec_ref[1:2, :], dvec_ref[2:3, :])) * gd
                 ).astype(yd_ref.dtype)
    hbuf[0:CONF_HALO, :] = hbuf[t:t + CONF_HALO, :]

    gb_ref[0] = _silu(_dot(hb, w_ref[:, C_BG:C_C])).astype(gb_ref.dtype)
    vt_ref[0] = lax.dot_general(wvt_ref[...], hb, _NT,
                                preferred_element_type=jnp.float32).astype(vt_ref.dtype)

    lf = _log_sigmoid(_dot(hb, w_ref[:, C_F:C_BG]) + fb_ref[...])
    r2 = lax.broadcasted_iota(jnp.int32, (CHUNK, CHUNK), 0)
    c2 = lax.broadcasted_iota(jnp.int32, (CHUNK, CHUNK), 1)
    tri = jnp.where(r2 >= c2, 1.0, 0.0).astype(jnp.bfloat16)
    carry = cum_carry[...]
    cums = []
    for c in range(t // CHUNK):
        hi, mid, lo = _split3(lf[c * CHUNK:(c + 1) * CHUNK])
        cc = (_dot(tri, hi.astype(jnp.bfloat16)) + _dot(tri, mid.astype(jnp.bfloat16))
              + _dot(tri, lo.astype(jnp.bfloat16))) + carry
        carry = cc[CHUNK - 1:CHUNK, :]
        cums.append(cc)
    cum_carry[...] = carry
    cum = jnp.concatenate(cums, axis=0) * LOG2E

    lane = lax.broadcasted_iota(jnp.int32, (t, AUG), 1)
    ones_q = jnp.where((lane >= HEAD_DIM + 3) & (lane < HEAD_DIM + 6), 1.0, 0.0)
    ones_k = jnp.where((lane >= HEAD_DIM) & (lane < HEAD_DIM + 3), 1.0, 0.0)
    for h in range(N_HEADS):
        hi, mid, lo = _split3(jnp.broadcast_to(cum[:, h:h + 1], (t, AUG)))
        eq = jnp.where(lane == HEAD_DIM, hi,
                       jnp.where(lane == HEAD_DIM + 1, mid,
                                 jnp.where(lane == HEAD_DIM + 2, lo, ones_q)))
        ek = jnp.where(lane == HEAD_DIM + 3, -hi,
                       jnp.where(lane == HEAD_DIM + 4, -mid,
                                 jnp.where(lane == HEAD_DIM + 5, -lo, ones_k)))
        qh = _dot(hb, w_ref[:, C_Q + h * AUG:C_Q + (h + 1) * AUG])
        kh = _dot(hb, w_ref[:, C_K + h * AUG:C_K + (h + 1) * AUG])
        q_ref[0, h] = (qh + eq).astype(q_ref.dtype)
        k_ref[0, h] = (kh + ek).astype(k_ref.dtype)


def _mixer_in(x, g, w1, wvt, fb, sguw, sgub, aln, scw, dww, dvec):
    bsz, seq, d = x.shape
    t = T_MIX
    nblk = seq // t
    bf = jnp.bfloat16
    const = lambda shape: pl.BlockSpec(shape, lambda b, j: (0,) * len(shape))
    tok = lambda width: pl.BlockSpec((1, t, width), lambda b, j: (b, j, 0))
    return pl.pallas_call(
        _mixer_in_kernel,
        grid=(bsz, nblk),
        in_specs=[
            tok(d), const(g.shape), const(w1.shape), const(wvt.shape), const(fb.shape),
            const(sguw.shape), const(sgub.shape), const(aln.shape), const(scw.shape),
            const(dww.shape), const(dvec.shape),
        ],
        out_specs=[
            tok(BR), tok(BR), tok(BR), tok(BR),
            pl.BlockSpec((1, N_HEADS, t, AUG), lambda b, j: (b, 0, j, 0)),
            pl.BlockSpec((1, N_HEADS, t, AUG), lambda b, j: (b, 0, j, 0)),
            pl.BlockSpec((1, BR, t), lambda b, j: (b, 0, j)),
        ],
        out_shape=[
            jax.ShapeDtypeStruct((bsz, seq, BR), bf),
            jax.ShapeDtypeStruct((bsz, seq, BR), bf),
            jax.ShapeDtypeStruct((bsz, seq, BR), bf),
            jax.ShapeDtypeStruct((bsz, seq, BR), bf),
            jax.ShapeDtypeStruct((bsz, N_HEADS, seq, AUG), bf),
            jax.ShapeDtypeStruct((bsz, N_HEADS, seq, AUG), bf),
            jax.ShapeDtypeStruct((bsz, BR, seq), bf),
        ],
        scratch_shapes=[
            pltpu.VMEM((SHORT_HALO + t, BR), jnp.float32),
            pltpu.VMEM((CONF_HALO + t, BR), jnp.float32),
            pltpu.VMEM((1, LANES), jnp.float32),
        ],
        compiler_params=pltpu.CompilerParams(
            dimension_semantics=("arbitrary", "arbitrary"), vmem_limit_bytes=VMEM_LIMIT),
        name="mixer_in",
    )(x, g, w1, wvt, fb, sguw, sgub, aln, scw, dww, dvec)


def _fox_attn_kernel(q_ref, k_ref, vt_ref, gb_ref, o_ref, m_sc, acc_sc):
    tq = q_ref.shape[2]
    tk = tq
    qi = pl.program_id(1)
    m_sc[...] = jnp.full(m_sc.shape, NEG, jnp.float32)
    acc_sc[...] = jnp.zeros(acc_sc.shape, jnp.float32)
    ones_rows = jnp.ones((V_ROWS - HEAD_DIM, tk), jnp.bfloat16)

    def step(kt, masked):
        ks = pl.multiple_of(kt * tk, tk)

        def scores(h):
            k = k_ref[0, h, pl.ds(ks, tk), :]
            s = lax.dot_general(k, q_ref[0, h], _NT,
                                preferred_element_type=jnp.float32)
            if masked:
                kpos = lax.broadcasted_iota(jnp.int32, s.shape, 0)
                qpos = lax.broadcasted_iota(jnp.int32, s.shape, 1)
                s = jnp.where(kpos <= qpos, s, NEG)
            return s

        def accumulate(h, s):
            m_old = m_sc[h]
            m_new = jnp.maximum(m_old, jnp.max(s, axis=0, keepdims=True))
            p = jnp.exp2(s - m_new).astype(jnp.bfloat16)
            vt = jnp.concatenate(
                [vt_ref[0, h * HEAD_DIM:(h + 1) * HEAD_DIM, pl.ds(ks, tk)], ones_rows], axis=0)
            acc_sc[h] = jnp.exp2(m_old - m_new) * acc_sc[h] + _dot(vt, p)
            m_sc[h] = m_new

        s_next = scores(0)
        for h in range(N_HEADS):
            s_cur = s_next
            if h + 1 < N_HEADS:
                s_next = scores(h + 1)
            accumulate(h, s_cur)

    def body(kt, carry):
        step(kt, False)
        return carry

    lax.fori_loop(0, qi, body, 0)
    step(qi, True)
    outs = []
    for h in range(N_HEADS):
        acc = acc_sc[h]
        outs.append(acc[:HEAD_DIM] / acc[HEAD_DIM:HEAD_DIM + 1])
    o = jnp.concatenate(outs, axis=0).T
    o_ref[0] = (o * gb_ref[0].astype(jnp.float32)).astype(o_ref.dtype)


def _fox_attn(q_aug, k_aug, vt, gb):
    bsz, nh, seq, aug = q_aug.shape
    tq = T_ATT
    return pl.pallas_call(
        _fox_attn_kernel,
        grid=(bsz, seq // tq),
        in_specs=[
            pl.BlockSpec((1, nh, tq, aug), lambda b, i: (b, 0, i, 0)),
            pl.BlockSpec((1, nh, seq, aug), lambda b, i: (b, 0, 0, 0)),
            pl.BlockSpec((1, BR, seq), lambda b, i: (b, 0, 0)),
            pl.BlockSpec((1, tq, BR), lambda b, i: (b, i, 0)),
        ],
        out_specs=pl.BlockSpec((1, tq, BR), lambda b, i: (b, i, 0)),
        out_shape=jax.ShapeDtypeStruct((bsz, seq, BR), jnp.bfloat16),
        scratch_shapes=[
            pltpu.VMEM((nh, 1, tq), jnp.float32),
            pltpu.VMEM((nh, V_ROWS, tq), jnp.float32),
        ],
        compiler_params=pltpu.CompilerParams(
            dimension_semantics=("arbitrary", "arbitrary"), vmem_limit_bytes=VMEM_LIMIT),
        name="fox_attn",
    )(q_aug, k_aug, vt, gb)


def _merge_out_kernel(x_ref, g_ref, ya_ref, yb_ref, yc_ref, yd_ref,
                      wm_ref, wb_ref, wo_ref, fg_ref, o_ref, *, final):
    x = x_ref[0]
    hb = (_rms_scale(x) * g_ref[...]).astype(jnp.bfloat16)
    ys = (ya_ref[0], yb_ref[0], yc_ref[0], yd_ref[0])
    merged = jnp.zeros(x.shape, jnp.float32)
    for n in range(N_BRANCH):
        gate = _sigmoid(_dot(hb, wm_ref[:, n * D_MODEL:(n + 1) * D_MODEL]))
        merged = merged + gate * _dot(ys[n], wb_ref[n])
    out = x + _dot(merged.astype(jnp.bfloat16), wo_ref[...])
    if final:
        out = _rms_scale(out) * fg_ref[...]
    o_ref[0] = out


def _merge_out(x, g, ya, yb, yc, yd, wm, wb, wo, fg, final):
    bsz, seq, d = x.shape
    t = T_OUT
    const = lambda shape: pl.BlockSpec(shape, lambda b, j: (0,) * len(shape))
    tok = lambda width: pl.BlockSpec((1, t, width), lambda b, j: (b, j, 0))
    return pl.pallas_call(
        functools.partial(_merge_out_kernel, final=final),
        grid=(bsz, seq // t),
        in_specs=[
            tok(d), const(g.shape), tok(BR), tok(BR), tok(BR), tok(BR),
            const(wm.shape), const(wb.shape), const(wo.shape), const(fg.shape),
        ],
        out_specs=tok(d),
        out_shape=jax.ShapeDtypeStruct((bsz, seq, d), jnp.float32),
        compiler_params=pltpu.CompilerParams(
            dimension_semantics=("arbitrary", "arbitrary"), vmem_limit_bytes=VMEM_LIMIT),
        name="merge_out_final" if final else "merge_out",
    )(x, g, ya, yb, yc, yd, wm, wb, wo, fg)


def _pack_mixer_weight(w_in):
    d = w_in.shape[0]
    a_uv, a_gate, qkv, f, b_gate, c_in, c_gate, d_glu, d_gate = (
        w_in[:, 0:512], w_in[:, 512:768], w_in[:, 768:1536], w_in[:, 1536:1540],
        w_in[:, 1540:1796], w_in[:, 1796:2564], w_in[:, 2564:2820], w_in[:, 2820:3332],
        w_in[:, 3332:3588])
    scale = LOG2E / math.sqrt(HEAD_DIM)
    pad = jnp.zeros((d, AUG - HEAD_DIM), w_in.dtype)
    q_cols, k_cols = [], []
    for h in range(N_HEADS):
        q_cols += [qkv[:, h * HEAD_DIM:(h + 1) * HEAD_DIM] * scale, pad]
        k_cols += [qkv[:, BR + h * HEAD_DIM:BR + (h + 1) * HEAD_DIM], pad]
    f_pad = jnp.concatenate([f, jnp.zeros((d, LANES - N_HEADS), w_in.dtype)], axis=1)
    w1 = jnp.concatenate([a_uv, a_gate] + q_cols + k_cols
                         + [f_pad, b_gate, c_in, c_gate, d_glu, d_gate], axis=1)
    wvt = qkv[:, 2 * BR:].T
    return w1.astype(jnp.bfloat16), wvt.astype(jnp.bfloat16)


def kernel(x, norm_g, w_in, f_bias, sgu_w, sgu_b, sgu_ln_g, sgu_ln_b, short_conv_w,
           conf_dw_w, conf_dw_b, conf_ln_g, conf_ln_b, w_branch, w_out, final_g):
    depth = norm_g.shape[0]
    bf = jnp.bfloat16
    fg = final_g.reshape(1, D_MODEL)
    for layer in range(depth):
        g = norm_g[layer].reshape(1, D_MODEL)
        w1, wvt = _pack_mixer_weight(w_in[layer])
        fb = jnp.zeros((1, LANES), jnp.float32).at[0, :N_HEADS].set(f_bias[layer])
        sguw = sgu_w[layer].reshape(N_HEADS * CHUNK, CHUNK)
        sgub = jnp.repeat(sgu_b[layer].T, HEAD_DIM, axis=1)
        aln = jnp.stack([sgu_ln_g[layer], sgu_ln_b[layer]])
        scw = jnp.zeros((SUBLANES, BR), jnp.float32).at[:SHORT_CONV].set(short_conv_w[layer])
        dww = jnp.zeros((CONF_HALO, BR), jnp.float32).at[:CONF_CONV].set(conf_dw_w[layer])
        dvec = jnp.stack([conf_dw_b[layer], conf_ln_g[layer], conf_ln_b[layer]])
        ya, yc, yd, gb, q_aug, k_aug, vt = _mixer_in(
            x, g, w1, wvt, fb, sguw, sgub, aln, scw, dww, dvec)
        yb = _fox_attn(q_aug, k_aug, vt, gb)
        wm = w_in[layer][:, 3588:].astype(bf)
        x = _merge_out(x, g, ya, yb, yc, yd, wm, w_branch[layer].astype(bf),
                       w_out[layer].astype(bf), fg, final=(layer == depth - 1))
    return x
```

```python
import functools
import math

import jax
import jax.numpy as jnp
from jax import lax
from jax.experimental import pallas as pl
from jax.experimental.pallas import tpu as pltpu

D_MODEL = 1024
N_BRANCH = 4
BR = D_MODEL // N_BRANCH
HEAD_DIM = 64
N_HEADS = BR // HEAD_DIM
CHUNK = 128
SHORT_CONV = 3
CONF_CONV = 31
EPS = 1e-6

LANES = 128
SUBLANES = 8
AUG = LANES
CONF_HALO = 32
SHORT_HALO = SUBLANES
CONV_ROWS = 64

T_MIX = 512
T_ATT = 512
V_ROWS = HEAD_DIM + 16
LOG2E = math.log2(math.e)
SCORE_LOOKAHEAD = 2
T_OUT = 512
VMEM_LIMIT = 56 * 1024 * 1024

C_UV, C_AG, C_Q, C_K, C_BG, C_C, C_CG, C_D, C_DG, C_END = (
    0, 512, 768, 1280, 1792, 2048, 2816, 3072, 3584, 3840)
F_LANE = HEAD_DIM

NEG = -1e30

_NT = (((1,), (1,)), ((), ()))


def _dot(a, b):
    return jnp.dot(a, b, preferred_element_type=jnp.float32)


def _sigmoid(x):
    return 0.5 + 0.5 * jnp.tanh(0.5 * x)


def _silu(x):
    h = 0.5 * x
    return h + h * jnp.tanh(h)


def _gelu_tanh(x):
    c = math.sqrt(2.0 / math.pi)
    return 0.5 * x * (1.0 + jnp.tanh(c * (x + 0.044715 * (x * x * x))))


def _log_sigmoid(x):
    return jnp.minimum(x, 0.0) - jnp.log(1.0 + jnp.exp(-jnp.abs(x)))


def _layernorm(x, g, b):
    mu = jnp.mean(x, axis=-1, keepdims=True)
    xc = x - mu
    var = jnp.mean(xc * xc, axis=-1, keepdims=True)
    return xc * lax.rsqrt(var + EPS) * g + b


def _rms_scale(x):
    return x * lax.rsqrt(jnp.mean(x * x, axis=-1, keepdims=True) + EPS)


def _split3(x):
    hi = x.astype(jnp.bfloat16).astype(jnp.float32)
    r = x - hi
    mid = r.astype(jnp.bfloat16).astype(jnp.float32)
    lo = r - mid
    return hi, mid, lo


def _mixer_in_kernel(x_ref, g_ref, w_ref, wvt_ref, fb_ref, sguw_ref, sgub_ref,
                     aln_ref, scw_ref, dww_ref, dvec_ref,
                     ya_ref, yc_ref, yd_ref, gb_ref, q_ref, k_ref, vt_ref,
                     zbuf, hbuf, cum_carry):
    t = x_ref.shape[1]
    j = pl.program_id(1)

    @pl.when(j == 0)
    def _():
        zbuf[0:SHORT_HALO, :] = jnp.zeros((SHORT_HALO, BR), jnp.float32)
        hbuf[0:CONF_HALO, :] = jnp.zeros((CONF_HALO, BR), jnp.float32)
        cum_carry[...] = jnp.zeros_like(cum_carry)

    x = x_ref[0]
    hb = (_rms_scale(x) * g_ref[...]).astype(jnp.bfloat16)

    gel = _gelu_tanh(_dot(hb, w_ref[:, C_UV:C_AG]))
    u = gel[:, :BR]
    v = _layernorm(gel[:, BR:], aln_ref[0:1, :], aln_ref[1:2, :])
    ga = _silu(_dot(hb, w_ref[:, C_AG:C_Q]))
    rows = lax.broadcasted_iota(jnp.int32, (N_HEADS * CHUNK, CHUNK), 0)
    cols = lax.broadcasted_iota(jnp.int32, (N_HEADS * CHUNK, CHUNK), 1)
    wst = jnp.where((rows % CHUNK) >= cols, sguw_ref[...], 0.0).astype(jnp.bfloat16)
    lane_head = lax.broadcasted_iota(jnp.int32, (CHUNK, BR), 1) // HEAD_DIM
    for c in range(t // CHUNK):
        sl = slice(c * CHUNK, (c + 1) * CHUNK)
        mall = _dot(wst, v[sl].astype(jnp.bfloat16))
        mixed = sgub_ref[...]
        for h in range(N_HEADS):
            mixed = mixed + jnp.where(lane_head == h, mall[h * CHUNK:(h + 1) * CHUNK], 0.0)
        ya_ref[0, sl, :] = (u[sl] * mixed * ga[sl]).astype(ya_ref.dtype)

    cin = _dot(hb, w_ref[:, C_C:C_CG])
    zbuf[SHORT_HALO:SHORT_HALO + t, :] = cin[:, BR:2 * BR] * cin[:, 2 * BR:]
    conv = jnp.zeros((t, BR), jnp.float32)
    for kk in range(SHORT_CONV):
        s = SHORT_CONV - 1 - kk
        conv = conv + zbuf[SHORT_HALO - s:SHORT_HALO - s + t, :] * scw_ref[kk:kk + 1, :]
    gc = _silu(_dot(hb, w_ref[:, C_CG:C_D]))
    yc_ref[0] = (cin[:, :BR] * conv * gc).astype(yc_ref.dtype)
    zbuf[0:SHORT_HALO, :] = zbuf[t:t + SHORT_HALO, :]

    glu = _dot(hb, w_ref[:, C_D:C_DG])
    hbuf[CONF_HALO:CONF_HALO + t, :] = glu[:, :BR] * _sigmoid(glu[:, BR:])
    gd = _silu(_dot(hb, w_ref[:, C_DG:C_END]))
    for r0 in range(0, t, CONV_ROWS):
        dconv = jnp.broadcast_to(dvec_ref[0:1, :], (CONV_ROWS, BR))
        win = hbuf[r0:r0 + CONF_HALO + CONV_ROWS, :]
        for b in range(SUBLANES):
            wb = win if b == 0 else pltpu.roll(win, b, 0)
            for a in range(CONF_HALO // SUBLANES):
                s = SUBLANES * a + b
                if s < CONF_CONV:
                    kk = CONF_CONV - 1 - s
                    off = CONF_HALO - SUBLANES * a
                    wk = jnp.concatenate([dww_ref[kk]] * (CONV_ROWS // SUBLANES), axis=0)
                    dconv = dconv + wb[off:off + CONV_ROWS, :] * wk
        yd_ref[0, r0:r0 + CONV_ROWS, :] = (
            _silu(_layernorm(dconv, dvec_ref[1:2, :], dvec_ref[2:3, :]))
            * gd[r0:r0 + CONV_ROWS]).astype(yd_ref.dtype)
    hbuf[0:CONF_HALO, :] = hbuf[t:t + CONF_HALO, :]

    gb_ref[0] = _silu(_dot(hb, w_ref[:, C_BG:C_C])).astype(gb_ref.dtype)
    vt_ref[0] = lax.dot_general(wvt_ref[...], hb, _NT,
                                preferred_element_type=jnp.float32).astype(vt_ref.dtype)

    qa = _dot(hb, w_ref[:, C_Q:C_K])
    ka = _dot(hb, w_ref[:, C_K:C_BG])

    lf = _log_sigmoid(qa[:, :AUG] + fb_ref[...])
    r2 = lax.broadcasted_iota(jnp.int32, (CHUNK, CHUNK), 0)
    c2 = lax.broadcasted_iota(jnp.int32, (CHUNK, CHUNK), 1)
    tri = jnp.where(r2 >= c2, 1.0, 0.0).astype(jnp.bfloat16)
    lane_c = lax.broadcasted_iota(jnp.int32, (CHUNK, LANES), 1)
    grp = (lane_c - F_LANE) // N_HEADS
    carry = cum_carry[...]
    cums = []
    for c in range(t // CHUNK):
        hi, mid, lo = _split3(lf[c * CHUNK:(c + 1) * CHUNK])
        packed = jnp.where(grp == 0, hi,
                           jnp.where(grp == 1, pltpu.roll(mid, N_HEADS, 1),
                                     jnp.where(grp == 2, pltpu.roll(lo, 2 * N_HEADS, 1), 0.0)))
        r = _dot(tri, packed.astype(jnp.bfloat16))
        cc = (r + pltpu.roll(r, LANES - N_HEADS, 1)
              + pltpu.roll(r, LANES - 2 * N_HEADS, 1)) + carry
        carry = cc[CHUNK - 1:CHUNK, :]
        cums.append(cc)
    cum_carry[...] = carry
    cum = jnp.concatenate(cums, axis=0) * LOG2E

    lane = lax.broadcasted_iota(jnp.int32, (t, AUG), 1)
    ones_q = jnp.where((lane >= HEAD_DIM + 3) & (lane < HEAD_DIM + 6), 1.0, 0.0)
    ones_k = jnp.where((lane >= HEAD_DIM) & (lane < HEAD_DIM + 3), 1.0, 0.0)
    for h in range(N_HEADS):
        hi, mid, lo = _split3(
            jnp.broadcast_to(cum[:, F_LANE + h:F_LANE + h + 1], (t, AUG)))
        eq = jnp.where(lane == HEAD_DIM, hi,
                       jnp.where(lane == HEAD_DIM + 1, mid,
                                 jnp.where(lane == HEAD_DIM + 2, lo, ones_q)))
        ek = jnp.where(lane == HEAD_DIM + 3, -hi,
                       jnp.where(lane == HEAD_DIM + 4, -mid,
                                 jnp.where(lane == HEAD_DIM + 5, -lo, ones_k)))
        qh = qa[:, h * AUG:(h + 1) * AUG]
        kh = ka[:, h * AUG:(h + 1) * AUG]
        q_ref[0, h] = jnp.where(lane < HEAD_DIM, qh, eq).astype(q_ref.dtype)
        k_ref[0, h] = jnp.where(lane < HEAD_DIM, kh, ek).astype(k_ref.dtype)


def _mixer_in(x, g, w1, wvt, fb, sguw, sgub, aln, scw, dww, dvec):
    bsz, seq, d = x.shape
    t = T_MIX
    nblk = seq // t
    bf = jnp.bfloat16
    const = lambda shape: pl.BlockSpec(shape, lambda b, j: (0,) * len(shape))
    tok = lambda width: pl.BlockSpec((1, t, width), lambda b, j: (b, j, 0))
    return pl.pallas_call(
        _mixer_in_kernel,
        grid=(bsz, nblk),
        in_specs=[
            tok(d), const(g.shape), const(w1.shape), const(wvt.shape), const(fb.shape),
            const(sguw.shape), const(sgub.shape), const(aln.shape), const(scw.shape),
            const(dww.shape), const(dvec.shape),
        ],
        out_specs=[
            tok(BR), tok(BR), tok(BR), tok(BR),
            pl.BlockSpec((1, N_HEADS, t, AUG), lambda b, j: (b, 0, j, 0)),
            pl.BlockSpec((1, N_HEADS, t, AUG), lambda b, j: (b, 0, j, 0)),
            pl.BlockSpec((1, BR, t), lambda b, j: (b, 0, j)),
        ],
        out_shape=[
            jax.ShapeDtypeStruct((bsz, seq, BR), bf),
            jax.ShapeDtypeStruct((bsz, seq, BR), bf),
            jax.ShapeDtypeStruct((bsz, seq, BR), bf),
            jax.ShapeDtypeStruct((bsz, seq, BR), bf),
            jax.ShapeDtypeStruct((bsz, N_HEADS, seq, AUG), bf),
            jax.ShapeDtypeStruct((bsz, N_HEADS, seq, AUG), bf),
            jax.ShapeDtypeStruct((bsz, BR, seq), bf),
        ],
        scratch_shapes=[
            pltpu.VMEM((SHORT_HALO + t, BR), jnp.float32),
            pltpu.VMEM((CONF_HALO + t, BR), jnp.float32),
            pltpu.VMEM((1, LANES), jnp.float32),
        ],
        compiler_params=pltpu.CompilerParams(
            dimension_semantics=("arbitrary", "arbitrary"), vmem_limit_bytes=VMEM_LIMIT),
        name="mixer_in",
    )(x, g, w1, wvt, fb, sguw, sgub, aln, scw, dww, dvec)


def _fox_attn_kernel(q_ref, k_ref, vt_ref, gb_ref, o_ref, m_sc, acc_sc, s_sc):
    tq = q_ref.shape[2]
    tk = tq
    qi = pl.program_id(1)
    m_sc[...] = jnp.full(m_sc.shape, NEG, jnp.float32)
    acc_sc[...] = jnp.zeros(acc_sc.shape, jnp.float32)
    ones_rows = jnp.ones((V_ROWS - HEAD_DIM, tk), jnp.bfloat16)

    def scores(h, kt):
        ks = pl.multiple_of(kt * tk, tk)
        k = k_ref[0, h, pl.ds(ks, tk), :]
        return lax.dot_general(k, q_ref[0, h], _NT,
                               preferred_element_type=jnp.float32)

    def causal(s):
        kpos = lax.broadcasted_iota(jnp.int32, s.shape, 0)
        qpos = lax.broadcasted_iota(jnp.int32, s.shape, 1)
        return jnp.where(kpos <= qpos, s, NEG)

    def accumulate(h, kt, s):
        ks = pl.multiple_of(kt * tk, tk)
        m_old = m_sc[h]
        m_new = jnp.maximum(m_old, jnp.max(s, axis=0, keepdims=True))
        p = jnp.exp2(s - m_new).astype(jnp.bfloat16)
        vt = jnp.concatenate(
            [vt_ref[0, h * HEAD_DIM:(h + 1) * HEAD_DIM, pl.ds(ks, tk)], ones_rows], axis=0)
        acc_sc[h] = jnp.exp2(m_old - m_new) * acc_sc[h] + _dot(vt, p)
        m_sc[h] = m_new

    def step(kt, last):
        todo = [(h, kt) for h in range(1, N_HEADS)] + ([] if last else [(0, kt + 1)])
        ready = [s_sc[...]]
        for h in range(N_HEADS):
            while todo and len(ready) < 1 + SCORE_LOOKAHEAD:
                ready.append(scores(*todo.pop(0)))
            s_cur = ready.pop(0)
            accumulate(h, kt, causal(s_cur) if last else s_cur)
        if not last:
            s_sc[...] = ready.pop(0)

    def body(kt, carry):
        step(kt, False)
        return carry

    s_sc[...] = scores(0, 0)
    lax.fori_loop(0, qi, body, 0)
    step(qi, True)
    outs = []
    for h in range(N_HEADS):
        acc = acc_sc[h]
        outs.append(acc[:HEAD_DIM] / acc[HEAD_DIM:HEAD_DIM + 1])
    o = jnp.concatenate(outs, axis=0).T
    o_ref[0] = (o * gb_ref[0].astype(jnp.float32)).astype(o_ref.dtype)


def _fox_attn(q_aug, k_aug, vt, gb):
    bsz, nh, seq, aug = q_aug.shape
    tq = T_ATT
    return pl.pallas_call(
        _fox_attn_kernel,
        grid=(bsz, seq // tq),
        in_specs=[
            pl.BlockSpec((1, nh, tq, aug), lambda b, i: (b, 0, i, 0)),
            pl.BlockSpec((1, nh, seq, aug), lambda b, i: (b, 0, 0, 0)),
            pl.BlockSpec((1, BR, seq), lambda b, i: (b, 0, 0)),
            pl.BlockSpec((1, tq, BR), lambda b, i: (b, i, 0)),
        ],
        out_specs=pl.BlockSpec((1, tq, BR), lambda b, i: (b, i, 0)),
        out_shape=jax.ShapeDtypeStruct((bsz, seq, BR), jnp.bfloat16),
        scratch_shapes=[
            pltpu.VMEM((nh, 1, tq), jnp.float32),
            pltpu.VMEM((nh, V_ROWS, tq), jnp.float32),
            pltpu.VMEM((tq, tq), jnp.float32),
        ],
        compiler_params=pltpu.CompilerParams(
            dimension_semantics=("arbitrary", "arbitrary"), vmem_limit_bytes=VMEM_LIMIT),
        name="fox_attn",
    )(q_aug, k_aug, vt, gb)


def _merge_out_kernel(x_ref, g_ref, ya_ref, yb_ref, yc_ref, yd_ref,
                      wm_ref, wb_ref, wo_ref, fg_ref, o_ref, *, final):
    x = x_ref[0]
    hb = (_rms_scale(x) * g_ref[...]).astype(jnp.bfloat16)
    ys = (ya_ref[0], yb_ref[0], yc_ref[0], yd_ref[0])
    merged = jnp.zeros(x.shape, jnp.float32)
    for n in range(N_BRANCH):
        gate = _sigmoid(_dot(hb, wm_ref[:, n * D_MODEL:(n + 1) * D_MODEL]))
        merged = merged + gate * _dot(ys[n], wb_ref[n])
    out = x + _dot(merged.astype(jnp.bfloat16), wo_ref[...])
    if final:
        out = _rms_scale(out) * fg_ref[...]
    o_ref[0] = out


def _merge_out(x, g, ya, yb, yc, yd, wm, wb, wo, fg, final):
    bsz, seq, d = x.shape
    t = T_OUT
    const = lambda shape: pl.BlockSpec(shape, lambda b, j: (0,) * len(shape))
    tok = lambda width: pl.BlockSpec((1, t, width), lambda b, j: (b, j, 0))
    return pl.pallas_call(
        functools.partial(_merge_out_kernel, final=final),
        grid=(bsz, seq // t),
        in_specs=[
            tok(d), const(g.shape), tok(BR), tok(BR), tok(BR), tok(BR),
            const(wm.shape), const(wb.shape), const(wo.shape), const(fg.shape),
        ],
        out_specs=tok(d),
        out_shape=jax.ShapeDtypeStruct((bsz, seq, d), jnp.float32),
        compiler_params=pltpu.CompilerParams(
            dimension_semantics=("arbitrary", "arbitrary"), vmem_limit_bytes=VMEM_LIMIT),
        name="merge_out_final" if final else "merge_out",
    )(x, g, ya, yb, yc, yd, wm, wb, wo, fg)


def _pack_mixer_weight(w_in):
    d = w_in.shape[0]
    bf = jnp.bfloat16
    wb = w_in[:, :3588].astype(bf)
    a_uv, a_gate, qkv, f, b_gate, c_in, c_gate, d_glu, d_gate = (
        wb[:, 0:512], wb[:, 512:768], wb[:, 768:1536], wb[:, 1536:1540],
        wb[:, 1540:1796], wb[:, 1796:2564], wb[:, 2564:2820], wb[:, 2820:3332],
        wb[:, 3332:3588])
    scale = LOG2E / math.sqrt(HEAD_DIM)
    q_scaled = (w_in[:, 768:768 + BR] * scale).astype(bf)
    pad = jnp.zeros((d, AUG - HEAD_DIM), bf)
    f_pad = jnp.concatenate([f, jnp.zeros((d, AUG - HEAD_DIM - N_HEADS), bf)], axis=1)
    q_cols, k_cols = [], []
    for h in range(N_HEADS):
        q_cols += [q_scaled[:, h * HEAD_DIM:(h + 1) * HEAD_DIM], f_pad if h == 0 else pad]
        k_cols += [qkv[:, BR + h * HEAD_DIM:BR + (h + 1) * HEAD_DIM], pad]
    w1 = jnp.concatenate([a_uv, a_gate] + q_cols + k_cols
                         + [b_gate, c_in, c_gate, d_glu, d_gate], axis=1)
    wvt = qkv[:, 2 * BR:].T
    return w1, wvt


def kernel(x, norm_g, w_in, f_bias, sgu_w, sgu_b, sgu_ln_g, sgu_ln_b, short_conv_w,
           conf_dw_w, conf_dw_b, conf_ln_g, conf_ln_b, w_branch, w_out, final_g):
    depth = norm_g.shape[0]
    bf = jnp.bfloat16
    fg = final_g.reshape(1, D_MODEL)
    for layer in range(depth):
        g = norm_g[layer].reshape(1, D_MODEL)
        w1, wvt = _pack_mixer_weight(w_in[layer])
        fb = jnp.zeros((1, LANES), jnp.float32).at[0, F_LANE:F_LANE + N_HEADS].set(f_bias[layer])
        sguw = sgu_w[layer].reshape(N_HEADS * CHUNK, CHUNK)
        sgub = jnp.repeat(sgu_b[layer].T, HEAD_DIM, axis=1)
        aln = jnp.stack([sgu_ln_g[layer], sgu_ln_b[layer]])
        scw = jnp.zeros((SUBLANES, BR), jnp.float32).at[:SHORT_CONV].set(short_conv_w[layer])
        dww = jnp.broadcast_to(conf_dw_w[layer][:, None, :], (CONF_CONV, SUBLANES, BR))
        dvec = jnp.stack([conf_dw_b[layer], conf_ln_g[layer], conf_ln_b[layer]])
        ya, yc, yd, gb, q_aug, k_aug, vt = _mixer_in(
            x, g, w1, wvt, fb, sguw, sgub, aln, scw, dww, dvec)
        yb = _fox_attn(q_aug, k_aug, vt, gb)
        wm = w_in[layer][:, 3588:].astype(bf)
        x = _merge_out(x, g, ya, yb, yc, yd, wm, w_branch[layer].astype(bf),
                       w_out[layer].astype(bf), fg, final=(layer == depth - 1))
    return x
```

```python
import functools
import math

import jax
import jax.numpy as jnp
from jax import lax
from jax.experimental import pallas as pl
from jax.experimental.pallas import tpu as pltpu

D_MODEL = 1024
N_BRANCH = 4
BR = D_MODEL // N_BRANCH
HEAD_DIM = 64
N_HEADS = BR // HEAD_DIM
CHUNK = 128
SHORT_CONV = 3
CONF_CONV = 31
EPS = 1e-6

LANES = 128
SUBLANES = 8
AUG = LANES
CONF_HALO = 32
SHORT_HALO = SUBLANES
CONV_ROWS = 64

T_MIX = 512
T_ATT = 512
V_ROWS = HEAD_DIM + 16
LOG2E = math.log2(math.e)
SCORE_LOOKAHEAD = 2
T_OUT = 512
VMEM_LIMIT = 56 * 1024 * 1024

W_LO = 768
W_HI = 1540
H_BG, H_C, H_CG, H_D, H_DG, H_MERGE = 0, 256, 1024, 1280, 1792, 2048
H_BLOCK = 2048
F_LANE = HEAD_DIM

NEG = -1e30

_NT = (((1,), (1,)), ((), ()))


def _dot(a, b):
    return jnp.dot(a, b, preferred_element_type=jnp.float32)


def _sigmoid(x):
    return 0.5 + 0.5 * jnp.tanh(0.5 * x)


def _silu(x):
    h = 0.5 * x
    return h + h * jnp.tanh(h)


def _gelu_tanh(x):
    c = math.sqrt(2.0 / math.pi)
    return 0.5 * x * (1.0 + jnp.tanh(c * (x + 0.044715 * (x * x * x))))


def _log_sigmoid(x):
    return jnp.minimum(x, 0.0) - jnp.log(1.0 + jnp.exp(-jnp.abs(x)))


def _layernorm(x, g, b):
    mu = jnp.mean(x, axis=-1, keepdims=True)
    xc = x - mu
    var = jnp.mean(xc * xc, axis=-1, keepdims=True)
    return xc * lax.rsqrt(var + EPS) * g + b


def _rms_scale(x):
    return x * lax.rsqrt(jnp.mean(x * x, axis=-1, keepdims=True) + EPS)


def _split3(x):
    hi = x.astype(jnp.bfloat16).astype(jnp.float32)
    r = x - hi
    mid = r.astype(jnp.bfloat16).astype(jnp.float32)
    lo = r - mid
    return hi, mid, lo


def _mixer_in_kernel(x_ref, g_ref, wa_ref, wqk_ref, wh_ref, wvt_ref, fb_ref, sguw_ref, sgub_ref,
                     aln_ref, scw_ref, dww_ref, dvec_ref,
                     ya_ref, yc_ref, yd_ref, gb_ref, q_ref, k_ref, vt_ref,
                     zbuf, hbuf, cum_carry):
    t = x_ref.shape[1]
    j = pl.program_id(1)

    @pl.when(j == 0)
    def _():
        zbuf[0:SHORT_HALO, :] = jnp.zeros((SHORT_HALO, BR), jnp.float32)
        hbuf[0:CONF_HALO, :] = jnp.zeros((CONF_HALO, BR), jnp.float32)
        cum_carry[...] = jnp.zeros_like(cum_carry)

    x = x_ref[0]
    hb = (_rms_scale(x) * g_ref[...]).astype(jnp.bfloat16)


    glu = _dot(hb, wh_ref[0, :, H_D:H_DG])
    gd_lin = _dot(hb, wh_ref[0, :, H_DG:H_MERGE])
    hbuf[CONF_HALO:CONF_HALO + t, :] = glu[:, :BR] * _sigmoid(glu[:, BR:])

    def conf_rows(r0):
        dconv = jnp.broadcast_to(dvec_ref[0:1, :], (CONV_ROWS, BR))
        win = hbuf[r0:r0 + CONF_HALO + CONV_ROWS, :]
        for b in range(SUBLANES):
            wb = win if b == 0 else pltpu.roll(win, b, 0)
            for a in range(CONF_HALO // SUBLANES):
                s = SUBLANES * a + b
                if s < CONF_CONV:
                    kk = CONF_CONV - 1 - s
                    off = CONF_HALO - SUBLANES * a
                    wk = jnp.concatenate([dww_ref[kk]] * (CONV_ROWS // SUBLANES), axis=0)
                    dconv = dconv + wb[off:off + CONV_ROWS, :] * wk
        yd_ref[0, r0:r0 + CONV_ROWS, :] = (
            _silu(_layernorm(dconv, dvec_ref[1:2, :], dvec_ref[2:3, :]))
            * _silu(gd_lin[r0:r0 + CONV_ROWS])).astype(yd_ref.dtype)

    qa = _dot(hb, wqk_ref[:, 0:N_HEADS * AUG])
    ka = _dot(hb, wqk_ref[:, N_HEADS * AUG:])
    for r0 in range(0, t // 2, CONV_ROWS):
        conf_rows(r0)

    lf = _log_sigmoid(qa[:, :AUG] + fb_ref[...])
    r2 = lax.broadcasted_iota(jnp.int32, (CHUNK, CHUNK), 0)
    c2 = lax.broadcasted_iota(jnp.int32, (CHUNK, CHUNK), 1)
    tri = jnp.where(r2 >= c2, 1.0, 0.0).astype(jnp.bfloat16)
    lane_c = lax.broadcasted_iota(jnp.int32, (CHUNK, LANES), 1)
    grp = (lane_c - F_LANE) // N_HEADS
    within = []
    for c in range(t // CHUNK):
        hi, mid, lo = _split3(lf[c * CHUNK:(c + 1) * CHUNK])
        packed = jnp.where(grp == 0, hi,
                           jnp.where(grp == 1, pltpu.roll(mid, N_HEADS, 1),
                                     jnp.where(grp == 2, pltpu.roll(lo, 2 * N_HEADS, 1), 0.0)))
        r = _dot(tri, packed.astype(jnp.bfloat16))
        within.append(r + pltpu.roll(r, LANES - N_HEADS, 1)
                      + pltpu.roll(r, LANES - 2 * N_HEADS, 1))

    uv = _dot(hb, wa_ref[0, :, 0:2 * BR])
    ga_lin = _dot(hb, wa_ref[0, :, 2 * BR:W_LO])
    for r0 in range(t // 2, t, CONV_ROWS):
        conf_rows(r0)
    hbuf[0:CONF_HALO, :] = hbuf[t:t + CONF_HALO, :]

    cin = _dot(hb, wh_ref[0, :, H_C:H_CG])
    gc_lin = _dot(hb, wh_ref[0, :, H_CG:H_D])

    carry = cum_carry[...]
    cums = []
    for cc in within:
        cums.append(cc + carry)
        carry = carry + cc[CHUNK - 1:CHUNK, :]
    cum_carry[...] = carry
    cum = jnp.concatenate(cums, axis=0) * LOG2E

    lane = lax.broadcasted_iota(jnp.int32, (t, AUG), 1)
    ones_q = jnp.where((lane >= HEAD_DIM + 3) & (lane < HEAD_DIM + 6), 1.0, 0.0)
    ones_k = jnp.where((lane >= HEAD_DIM) & (lane < HEAD_DIM + 3), 1.0, 0.0)
    for h in range(N_HEADS):
        hi, mid, lo = _split3(
            jnp.broadcast_to(cum[:, F_LANE + h:F_LANE + h + 1], (t, AUG)))
        eq = jnp.where(lane == HEAD_DIM, hi,
                       jnp.where(lane == HEAD_DIM + 1, mid,
                                 jnp.where(lane == HEAD_DIM + 2, lo, ones_q)))
        ek = jnp.where(lane == HEAD_DIM + 3, -hi,
                       jnp.where(lane == HEAD_DIM + 4, -mid,
                                 jnp.where(lane == HEAD_DIM + 5, -lo, ones_k)))
        qh = qa[:, h * AUG:(h + 1) * AUG]
        kh = ka[:, h * AUG:(h + 1) * AUG]
        q_ref[0, h] = jnp.where(lane < HEAD_DIM, qh, eq).astype(q_ref.dtype)
        k_ref[0, h] = jnp.where(lane < HEAD_DIM, kh, ek).astype(k_ref.dtype)

    gb_lin = _dot(hb, wh_ref[0, :, H_BG:H_C])
    vt = lax.dot_general(wvt_ref[...], hb, _NT, preferred_element_type=jnp.float32)

    gel = _gelu_tanh(uv)
    u = gel[:, :BR]
    v = _layernorm(gel[:, BR:], aln_ref[0:1, :], aln_ref[1:2, :])
    rows = lax.broadcasted_iota(jnp.int32, (N_HEADS * CHUNK, CHUNK), 0)
    cols = lax.broadcasted_iota(jnp.int32, (N_HEADS * CHUNK, CHUNK), 1)
    wst = jnp.where((rows % CHUNK) >= cols, sguw_ref[...], 0.0).astype(jnp.bfloat16)
    malls = [_dot(wst, v[c * CHUNK:(c + 1) * CHUNK].astype(jnp.bfloat16))
             for c in range(t // CHUNK)]

    zbuf[SHORT_HALO:SHORT_HALO + t, :] = cin[:, BR:2 * BR] * cin[:, 2 * BR:]
    conv = jnp.zeros((t, BR), jnp.float32)
    for kk in range(SHORT_CONV):
        s = SHORT_CONV - 1 - kk
        conv = conv + zbuf[SHORT_HALO - s:SHORT_HALO - s + t, :] * scw_ref[kk:kk + 1, :]
    yc_ref[0] = (cin[:, :BR] * conv * _silu(gc_lin)).astype(yc_ref.dtype)
    zbuf[0:SHORT_HALO, :] = zbuf[t:t + SHORT_HALO, :]

    gb_ref[0] = _silu(gb_lin).astype(gb_ref.dtype)
    vt_ref[0] = vt.astype(vt_ref.dtype)

    lane_head = lax.broadcasted_iota(jnp.int32, (CHUNK, BR), 1) // HEAD_DIM
    ga = _silu(ga_lin)
    for c in range(t // CHUNK):
        sl = slice(c * CHUNK, (c + 1) * CHUNK)
        mixed = sgub_ref[...]
        for h in range(N_HEADS):
            mixed = mixed + jnp.where(lane_head == h, malls[c][h * CHUNK:(h + 1) * CHUNK], 0.0)
        ya_ref[0, sl, :] = (u[sl] * mixed * ga[sl]).astype(ya_ref.dtype)


def _mixer_in(layer, x, g, w_bf, wqk, w_hi, wvt, fb, sguw, sgub, aln, scw, dww, dvec):
    bsz, seq, d = x.shape
    t = T_MIX
    nblk = seq // t
    bf = jnp.bfloat16
    const = lambda shape: pl.BlockSpec(shape, lambda b, j: (0,) * len(shape))
    tok = lambda width: pl.BlockSpec((1, t, width), lambda b, j: (b, j, 0))
    return pl.pallas_call(
        _mixer_in_kernel,
        grid=(bsz, nblk),
        in_specs=[
            tok(d), const(g.shape),
            pl.BlockSpec((1, d, W_LO), lambda b, j: (layer, 0, 0)),
            const(wqk.shape),
            pl.BlockSpec((1, d, H_BLOCK), lambda b, j: (layer, 0, 0)),
            const(wvt.shape), const(fb.shape),
            const(sguw.shape), const(sgub.shape), const(aln.shape), const(scw.shape),
            const(dww.shape), const(dvec.shape),
        ],
        out_specs=[
            tok(BR), tok(BR), tok(BR), tok(BR),
            pl.BlockSpec((1, N_HEADS, t, AUG), lambda b, j: (b, 0, j, 0)),
            pl.BlockSpec((1, N_HEADS, t, AUG), lambda b, j: (b, 0, j, 0)),
            pl.BlockSpec((1, BR, t), lambda b, j: (b, 0, j)),
        ],
        out_shape=[
            jax.ShapeDtypeStruct((bsz, seq, BR), bf),
            jax.ShapeDtypeStruct((bsz, seq, BR), bf),
            jax.ShapeDtypeStruct((bsz, seq, BR), bf),
            jax.ShapeDtypeStruct((bsz, seq, BR), bf),
            jax.ShapeDtypeStruct((bsz, N_HEADS, seq, AUG), bf),
            jax.ShapeDtypeStruct((bsz, N_HEADS, seq, AUG), bf),
            jax.ShapeDtypeStruct((bsz, BR, seq), bf),
        ],
        scratch_shapes=[
            pltpu.VMEM((SHORT_HALO + t, BR), jnp.float32),
            pltpu.VMEM((CONF_HALO + t, BR), jnp.float32),
            pltpu.VMEM((1, LANES), jnp.float32),
        ],
        compiler_params=pltpu.CompilerParams(
            dimension_semantics=("arbitrary", "arbitrary"), vmem_limit_bytes=VMEM_LIMIT),
        name="mixer_in",
    )(x, g, w_bf, wqk, w_hi, wvt, fb, sguw, sgub, aln, scw, dww, dvec)


def _fox_attn_kernel(q_ref, k_ref, vt_ref, gb_ref, o_ref, m_sc, acc_sc, s_sc):
    tq = q_ref.shape[2]
    tk = tq
    qi = pl.program_id(1)
    m_sc[...] = jnp.full(m_sc.shape, NEG, jnp.float32)
    acc_sc[...] = jnp.zeros(acc_sc.shape, jnp.float32)
    ones_rows = jnp.ones((V_ROWS - HEAD_DIM, tk), jnp.bfloat16)

    def scores(h, kt):
        ks = pl.multiple_of(kt * tk, tk)
        k = k_ref[0, h, pl.ds(ks, tk), :]
        return lax.dot_general(k, q_ref[0, h], _NT,
                               preferred_element_type=jnp.float32)

    def causal(s):
        kpos = lax.broadcasted_iota(jnp.int32, s.shape, 0)
        qpos = lax.broadcasted_iota(jnp.int32, s.shape, 1)
        return jnp.where(kpos <= qpos, s, NEG)

    def accumulate(h, kt, s):
        ks = pl.multiple_of(kt * tk, tk)
        m_old = m_sc[h]
        m_new = jnp.maximum(m_old, jnp.max(s, axis=0, keepdims=True))
        p = jnp.exp2(s - m_new).astype(jnp.bfloat16)
        vt = jnp.concatenate(
            [vt_ref[0, h * HEAD_DIM:(h + 1) * HEAD_DIM, pl.ds(ks, tk)], ones_rows], axis=0)
        acc_sc[h] = jnp.exp2(m_old - m_new) * acc_sc[h] + _dot(vt, p)
        m_sc[h] = m_new

    def step(kt, last):
        todo = [(h, kt) for h in range(1, N_HEADS)] + ([] if last else [(0, kt + 1)])
        ready = [s_sc[...]]
        for h in range(N_HEADS):
            while todo and len(ready) < 1 + SCORE_LOOKAHEAD:
                ready.append(scores(*todo.pop(0)))
            s_cur = ready.pop(0)
            accumulate(h, kt, causal(s_cur) if last else s_cur)
        if not last:
            s_sc[...] = ready.pop(0)

    def body(kt, carry):
        step(kt, False)
        return carry

    s_sc[...] = scores(0, 0)
    lax.fori_loop(0, qi, body, 0)
    step(qi, True)
    outs = []
    for h in range(N_HEADS):
        acc = acc_sc[h]
        outs.append(acc[:HEAD_DIM] / acc[HEAD_DIM:HEAD_DIM + 1])
    o = jnp.concatenate(outs, axis=0).T
    o_ref[0] = (o * gb_ref[0].astype(jnp.float32)).astype(o_ref.dtype)


def _fox_attn(q_aug, k_aug, vt, gb):
    bsz, nh, seq, aug = q_aug.shape
    tq = T_ATT
    return pl.pallas_call(
        _fox_attn_kernel,
        grid=(bsz, seq // tq),
        in_specs=[
            pl.BlockSpec((1, nh, tq, aug), lambda b, i: (b, 0, i, 0)),
            pl.BlockSpec((1, nh, seq, aug), lambda b, i: (b, 0, 0, 0)),
            pl.BlockSpec((1, BR, seq), lambda b, i: (b, 0, 0)),
            pl.BlockSpec((1, tq, BR), lambda b, i: (b, i, 0)),
        ],
        out_specs=pl.BlockSpec((1, tq, BR), lambda b, i: (b, i, 0)),
        out_shape=jax.ShapeDtypeStruct((bsz, seq, BR), jnp.bfloat16),
        scratch_shapes=[
            pltpu.VMEM((nh, 1, tq), jnp.float32),
            pltpu.VMEM((nh, V_ROWS, tq), jnp.float32),
            pltpu.VMEM((tq, tq), jnp.float32),
        ],
        compiler_params=pltpu.CompilerParams(
            dimension_semantics=("arbitrary", "arbitrary"), vmem_limit_bytes=VMEM_LIMIT),
        name="fox_attn",
    )(q_aug, k_aug, vt, gb)


def _merge_out_kernel(x_ref, g_ref, ya_ref, yb_ref, yc_ref, yd_ref,
                      wm0_ref, wm1_ref, wb_ref, wo_ref, fg_ref, o_ref, *, final):
    x = x_ref[0]
    hb = (_rms_scale(x) * g_ref[...]).astype(jnp.bfloat16)
    ys = (ya_ref[0], yb_ref[0], yc_ref[0], yd_ref[0])
    per_ref = H_BLOCK // D_MODEL
    merged = jnp.zeros(x.shape, jnp.float32)
    for n in range(N_BRANCH):
        wm_ref = (wm0_ref, wm1_ref)[n // per_ref]
        c0 = (n % per_ref) * D_MODEL
        gate = _sigmoid(_dot(hb, wm_ref[0, :, c0:c0 + D_MODEL]))
        merged = merged + gate * _dot(ys[n], wb_ref[n])
    out = x + _dot(merged.astype(jnp.bfloat16), wo_ref[...])
    if final:
        out = _rms_scale(out) * fg_ref[...]
    o_ref[0] = out


def _merge_out(layer, x, g, ya, yb, yc, yd, w_hi, wb, wo, fg, final):
    bsz, seq, d = x.shape
    t = T_OUT
    const = lambda shape: pl.BlockSpec(shape, lambda b, j: (0,) * len(shape))
    tok = lambda width: pl.BlockSpec((1, t, width), lambda b, j: (b, j, 0))
    first = H_MERGE // H_BLOCK
    return pl.pallas_call(
        functools.partial(_merge_out_kernel, final=final),
        grid=(bsz, seq // t),
        in_specs=[
            tok(d), const(g.shape), tok(BR), tok(BR), tok(BR), tok(BR),
            pl.BlockSpec((1, d, H_BLOCK), lambda b, j: (layer, 0, first)),
            pl.BlockSpec((1, d, H_BLOCK), lambda b, j: (layer, 0, first + 1)),
            const(wb.shape), const(wo.shape), const(fg.shape),
        ],
        out_specs=tok(d),
        out_shape=jax.ShapeDtypeStruct((bsz, seq, d), jnp.float32),
        compiler_params=pltpu.CompilerParams(
            dimension_semantics=("arbitrary", "arbitrary"), vmem_limit_bytes=VMEM_LIMIT),
        name="merge_out_final" if final else "merge_out",
    )(x, g, ya, yb, yc, yd, w_hi, w_hi, wb, wo, fg)


def _pack_qkv_weight(w_qkvf):
    d = w_qkvf.shape[0]
    bf = jnp.bfloat16
    q, k, v, f = (w_qkvf[:, :BR], w_qkvf[:, BR:2 * BR], w_qkvf[:, 2 * BR:3 * BR],
                  w_qkvf[:, 3 * BR:])
    q = q * (LOG2E / math.sqrt(HEAD_DIM))
    pad = jnp.zeros((d, AUG - HEAD_DIM), w_qkvf.dtype)
    f_pad = jnp.concatenate([f, jnp.zeros((d, AUG - HEAD_DIM - N_HEADS), w_qkvf.dtype)], axis=1)
    q_cols, k_cols = [], []
    for h in range(N_HEADS):
        q_cols += [q[:, h * HEAD_DIM:(h + 1) * HEAD_DIM], f_pad if h == 0 else pad]
        k_cols += [k[:, h * HEAD_DIM:(h + 1) * HEAD_DIM], pad]
    return jnp.concatenate(q_cols + k_cols, axis=1).astype(bf), v.T.astype(bf)


def kernel(x, norm_g, w_in, f_bias, sgu_w, sgu_b, sgu_ln_g, sgu_ln_b, short_conv_w,
           conf_dw_w, conf_dw_b, conf_ln_g, conf_ln_b, w_branch, w_out, final_g):
    depth = norm_g.shape[0]
    bf = jnp.bfloat16
    fg = final_g.reshape(1, D_MODEL)
    w_bf = w_in.astype(bf)
    w_hi = w_bf[:, :, W_HI:]
    wb_bf = w_branch.astype(bf)
    wo_bf = w_out.astype(bf)
    for layer in range(depth):
        g = norm_g[layer].reshape(1, D_MODEL)
        wqk, wvt = _pack_qkv_weight(w_in[layer][:, W_LO:W_HI])
        fb = jnp.zeros((1, LANES), jnp.float32).at[0, F_LANE:F_LANE + N_HEADS].set(f_bias[layer])
        sguw = sgu_w[layer].reshape(N_HEADS * CHUNK, CHUNK)
        sgub = jnp.repeat(sgu_b[layer].T, HEAD_DIM, axis=1)
        aln = jnp.stack([sgu_ln_g[layer], sgu_ln_b[layer]])
        scw = jnp.zeros((SUBLANES, BR), jnp.float32).at[:SHORT_CONV].set(short_conv_w[layer])
        dww = jnp.broadcast_to(conf_dw_w[layer][:, None, :], (CONF_CONV, SUBLANES, BR))
        dvec = jnp.stack([conf_dw_b[layer], conf_ln_g[layer], conf_ln_b[layer]])
        ya, yc, yd, gb, q_aug, k_aug, vt = _mixer_in(
            layer, x, g, w_bf, wqk, w_hi, wvt, fb, sguw, sgub, aln, scw, dww, dvec)
        yb = _fox_attn(q_aug, k_aug, vt, gb)
        x = _merge_out(layer, x, g, ya, yb, yc, yd, w_hi, wb_bf[layer], wo_bf[layer], fg,
                       final=(layer == depth - 1))
    return x
```

```python
import functools
import math

import jax
import jax.numpy as jnp
from jax import lax
from jax.experimental import pallas as pl
from jax.experimental.pallas import tpu as pltpu

D_MODEL = 1024
N_BRANCH = 4
BR = D_MODEL // N_BRANCH
HEAD_DIM = 64
N_HEADS = BR // HEAD_DIM
CHUNK = 128
SHORT_CONV = 3
CONF_CONV = 31
EPS = 1e-6

LANES = 128
SUBLANES = 8
AUG = LANES
CONF_HALO = 32
SHORT_HALO = SUBLANES
CONV_ROWS = 64

T_MIX = 512
T_ATT = 512
V_ROWS = HEAD_DIM + 16
LOG2E = math.log2(math.e)
SCORE_LOOKAHEAD = 2
T_OUT = 512
PREP_ROWS = 128
VMEM_LIMIT = 56 * 1024 * 1024

W_LO = 768
W_HI = 1540
H_BG, H_C, H_CG, H_D, H_DG, H_MERGE = 0, 256, 1024, 1280, 1792, 2048
H_BLOCK = 2048
F_HEAD, F_LANE = 1, 0

NEG = -1e30

_NT = (((1,), (1,)), ((), ()))


def _dot(a, b):
    return jnp.dot(a, b, preferred_element_type=jnp.float32)


def _sigmoid(x):
    return 0.5 + 0.5 * jnp.tanh(0.5 * x)


def _silu(x):
    h = 0.5 * x
    return h + h * jnp.tanh(h)


def _gelu_tanh(x):
    c = math.sqrt(2.0 / math.pi)
    return 0.5 * x * (1.0 + jnp.tanh(c * (x + 0.044715 * (x * x * x))))


def _log_sigmoid(x):
    return jnp.minimum(x, 0.0) - jnp.log(1.0 + jnp.exp(-jnp.abs(x)))


def _layernorm(x, g, b):
    mu = jnp.mean(x, axis=-1, keepdims=True)
    xc = x - mu
    var = jnp.mean(xc * xc, axis=-1, keepdims=True)
    return xc * lax.rsqrt(var + EPS) * g + b


def _rms_scale(x):
    return x * lax.rsqrt(jnp.mean(x * x, axis=-1, keepdims=True) + EPS)


def _split3(x):
    hi = x.astype(jnp.bfloat16).astype(jnp.float32)
    r = x - hi
    mid = r.astype(jnp.bfloat16).astype(jnp.float32)
    lo = r - mid
    return hi, mid, lo


def _mixer_in_kernel(x_ref, g_ref, wa_ref, wqk_ref, wh_ref, wvt_ref, fb_ref, sguw_ref, sgub_ref,
                     aln_ref, scw_ref, dww_ref, dvec_ref,
                     ya_ref, yc_ref, yd_ref, gb_ref, q_ref, k_ref, vt_ref,
                     zbuf, hbuf, cum_carry):
    t = x_ref.shape[1]
    j = pl.program_id(1)

    @pl.when(j == 0)
    def _():
        zbuf[0:SHORT_HALO, :] = jnp.zeros((SHORT_HALO, BR), jnp.float32)
        hbuf[0:CONF_HALO, :] = jnp.zeros((CONF_HALO, BR), jnp.float32)
        cum_carry[...] = jnp.zeros_like(cum_carry)

    x = x_ref[0]
    hb = (_rms_scale(x) * g_ref[...]).astype(jnp.bfloat16)


    glu = _dot(hb, wh_ref[0, :, H_D:H_DG])
    gd_lin = _dot(hb, wh_ref[0, :, H_DG:H_MERGE])
    hbuf[CONF_HALO:CONF_HALO + t, :] = glu[:, :BR] * _sigmoid(glu[:, BR:])

    def conf_rows(r0):
        dconv = jnp.broadcast_to(dvec_ref[0:1, :], (CONV_ROWS, BR))
        win = hbuf[r0:r0 + CONF_HALO + CONV_ROWS, :]
        for b in range(SUBLANES):
            wb = win if b == 0 else pltpu.roll(win, b, 0)
            for a in range(CONF_HALO // SUBLANES):
                s = SUBLANES * a + b
                if s < CONF_CONV:
                    kk = CONF_CONV - 1 - s
                    off = CONF_HALO - SUBLANES * a
                    wk = jnp.concatenate([dww_ref[kk]] * (CONV_ROWS // SUBLANES), axis=0)
                    dconv = dconv + wb[off:off + CONV_ROWS, :] * wk
        yd_ref[0, r0:r0 + CONV_ROWS, :] = (
            _silu(_layernorm(dconv, dvec_ref[1:2, :], dvec_ref[2:3, :]))
            * _silu(gd_lin[r0:r0 + CONV_ROWS])).astype(yd_ref.dtype)

    qa = _dot(hb, wqk_ref[0, :, 0:N_HEADS * AUG])
    ka = _dot(hb, wqk_ref[0, :, N_HEADS * AUG:])
    for r0 in range(0, t // 2, CONV_ROWS):
        conf_rows(r0)

    lf = _log_sigmoid(qa[:, F_HEAD * AUG:(F_HEAD + 1) * AUG] + fb_ref[...])
    r2 = lax.broadcasted_iota(jnp.int32, (CHUNK, CHUNK), 0)
    c2 = lax.broadcasted_iota(jnp.int32, (CHUNK, CHUNK), 1)
    tri = jnp.where(r2 >= c2, 1.0, 0.0).astype(jnp.bfloat16)
    lane_c = lax.broadcasted_iota(jnp.int32, (CHUNK, LANES), 1)
    grp = (lane_c - F_LANE) // N_HEADS
    within = []
    for c in range(t // CHUNK):
        hi, mid, lo = _split3(lf[c * CHUNK:(c + 1) * CHUNK])
        packed = jnp.where(grp == 0, hi,
                           jnp.where(grp == 1, pltpu.roll(mid, N_HEADS, 1),
                                     jnp.where(grp == 2, pltpu.roll(lo, 2 * N_HEADS, 1), 0.0)))
        r = _dot(tri, packed.astype(jnp.bfloat16))
        within.append(r + pltpu.roll(r, LANES - N_HEADS, 1)
                      + pltpu.roll(r, LANES - 2 * N_HEADS, 1))

    uv = _dot(hb, wa_ref[0, :, 0:2 * BR])
    ga_lin = _dot(hb, wa_ref[0, :, 2 * BR:W_LO])
    for r0 in range(t // 2, t, CONV_ROWS):
        conf_rows(r0)
    hbuf[0:CONF_HALO, :] = hbuf[t:t + CONF_HALO, :]

    cin = _dot(hb, wh_ref[0, :, H_C:H_CG])
    gc_lin = _dot(hb, wh_ref[0, :, H_CG:H_D])

    carry = cum_carry[...]
    cums = []
    for cc in within:
        cums.append(cc + carry)
        carry = carry + cc[CHUNK - 1:CHUNK, :]
    cum_carry[...] = carry
    cum = jnp.concatenate(cums, axis=0) * LOG2E

    lane = lax.broadcasted_iota(jnp.int32, (t, AUG), 1)
    for h in range(N_HEADS):
        d0, a0 = _head_lanes(h)
        in_head = (lane >= d0) & (lane < d0 + HEAD_DIM)
        ones_q = jnp.where((lane >= a0 + 3) & (lane < a0 + 6), 1.0, 0.0)
        ones_k = jnp.where((lane >= a0) & (lane < a0 + 3), 1.0, 0.0)
        hi, mid, lo = _split3(
            jnp.broadcast_to(cum[:, F_LANE + h:F_LANE + h + 1], (t, AUG)))
        eq = jnp.where(lane == a0, hi,
                       jnp.where(lane == a0 + 1, mid,
                                 jnp.where(lane == a0 + 2, lo, ones_q)))
        ek = jnp.where(lane == a0 + 3, -hi,
                       jnp.where(lane == a0 + 4, -mid,
                                 jnp.where(lane == a0 + 5, -lo, ones_k)))
        qh = qa[:, h * AUG:(h + 1) * AUG]
        kh = ka[:, h * AUG:(h + 1) * AUG]
        q_ref[0, h] = jnp.where(in_head, qh, eq).astype(q_ref.dtype)
        k_ref[0, h] = jnp.where(in_head, kh, ek).astype(k_ref.dtype)

    gb_lin = _dot(hb, wh_ref[0, :, H_BG:H_C])
    vt = lax.dot_general(wvt_ref[0], hb, _NT, preferred_element_type=jnp.float32)

    gel = _gelu_tanh(uv)
    u = gel[:, :BR]
    v = _layernorm(gel[:, BR:], aln_ref[0:1, :], aln_ref[1:2, :])
    rows = lax.broadcasted_iota(jnp.int32, (N_HEADS * CHUNK, CHUNK), 0)
    cols = lax.broadcasted_iota(jnp.int32, (N_HEADS * CHUNK, CHUNK), 1)
    wst = jnp.where((rows % CHUNK) >= cols, sguw_ref[...], 0.0).astype(jnp.bfloat16)
    malls = [_dot(wst, v[c * CHUNK:(c + 1) * CHUNK].astype(jnp.bfloat16))
             for c in range(t // CHUNK)]

    zbuf[SHORT_HALO:SHORT_HALO + t, :] = cin[:, BR:2 * BR] * cin[:, 2 * BR:]
    conv = jnp.zeros((t, BR), jnp.float32)
    for kk in range(SHORT_CONV):
        s = SHORT_CONV - 1 - kk
        conv = conv + zbuf[SHORT_HALO - s:SHORT_HALO - s + t, :] * scw_ref[kk:kk + 1, :]
    yc_ref[0] = (cin[:, :BR] * conv * _silu(gc_lin)).astype(yc_ref.dtype)
    zbuf[0:SHORT_HALO, :] = zbuf[t:t + SHORT_HALO, :]

    gb_ref[0] = _silu(gb_lin).astype(gb_ref.dtype)
    vt_ref[0] = vt.astype(vt_ref.dtype)

    lane_head = lax.broadcasted_iota(jnp.int32, (CHUNK, BR), 1) // HEAD_DIM
    ga = _silu(ga_lin)
    for c in range(t // CHUNK):
        sl = slice(c * CHUNK, (c + 1) * CHUNK)
        mixed = sgub_ref[...]
        for h in range(N_HEADS):
            mixed = mixed + jnp.where(lane_head == h, malls[c][h * CHUNK:(h + 1) * CHUNK], 0.0)
        ya_ref[0, sl, :] = (u[sl] * mixed * ga[sl]).astype(ya_ref.dtype)


def _mixer_in(layer, x, g, w_bf, wqk, w_hi, wvt, fb, sguw, sgub, aln, scw, dww, dvec):
    bsz, seq, d = x.shape
    t = T_MIX
    nblk = seq // t
    bf = jnp.bfloat16
    const = lambda shape: pl.BlockSpec(shape, lambda b, j: (0,) * len(shape))
    tok = lambda width: pl.BlockSpec((1, t, width), lambda b, j: (b, j, 0))
    return pl.pallas_call(
        _mixer_in_kernel,
        grid=(bsz, nblk),
        in_specs=[
            tok(d), const(g.shape),
            pl.BlockSpec((1, d, W_LO), lambda b, j: (layer, 0, 0)),
            pl.BlockSpec((1, d, 2 * N_HEADS * AUG), lambda b, j: (layer, 0, 0)),
            pl.BlockSpec((1, d, H_BLOCK), lambda b, j: (layer, 0, 0)),
            pl.BlockSpec((1, BR, d), lambda b, j: (layer, 0, 0)), const(fb.shape),
            const(sguw.shape), const(sgub.shape), const(aln.shape), const(scw.shape),
            const(dww.shape), const(dvec.shape),
        ],
        out_specs=[
            tok(BR), tok(BR), tok(BR), tok(BR),
            pl.BlockSpec((1, N_HEADS, t, AUG), lambda b, j: (b, 0, j, 0)),
            pl.BlockSpec((1, N_HEADS, t, AUG), lambda b, j: (b, 0, j, 0)),
            pl.BlockSpec((1, BR, t), lambda b, j: (b, 0, j)),
        ],
        out_shape=[
            jax.ShapeDtypeStruct((bsz, seq, BR), bf),
            jax.ShapeDtypeStruct((bsz, seq, BR), bf),
            jax.ShapeDtypeStruct((bsz, seq, BR), bf),
            jax.ShapeDtypeStruct((bsz, seq, BR), bf),
            jax.ShapeDtypeStruct((bsz, N_HEADS, seq, AUG), bf),
            jax.ShapeDtypeStruct((bsz, N_HEADS, seq, AUG), bf),
            jax.ShapeDtypeStruct((bsz, BR, seq), bf),
        ],
        scratch_shapes=[
            pltpu.VMEM((SHORT_HALO + t, BR), jnp.float32),
            pltpu.VMEM((CONF_HALO + t, BR), jnp.float32),
            pltpu.VMEM((1, LANES), jnp.float32),
        ],
        compiler_params=pltpu.CompilerParams(
            dimension_semantics=("arbitrary", "arbitrary"), vmem_limit_bytes=VMEM_LIMIT),
        name="mixer_in",
    )(x, g, w_bf, wqk, w_hi, wvt, fb, sguw, sgub, aln, scw, dww, dvec)


def _fox_attn_kernel(q_ref, k_ref, vt_ref, gb_ref, o_ref, m_sc, acc_sc, s_sc):
    tq = q_ref.shape[2]
    tk = tq
    qi = pl.program_id(1)
    m_sc[...] = jnp.full(m_sc.shape, NEG, jnp.float32)
    acc_sc[...] = jnp.zeros(acc_sc.shape, jnp.float32)
    ones_rows = jnp.ones((V_ROWS - HEAD_DIM, tk), jnp.bfloat16)

    def scores(h, kt):
        ks = pl.multiple_of(kt * tk, tk)
        k = k_ref[0, h, pl.ds(ks, tk), :]
        return lax.dot_general(k, q_ref[0, h], _NT,
                               preferred_element_type=jnp.float32)

    def causal(s):
        kpos = lax.broadcasted_iota(jnp.int32, s.shape, 0)
        qpos = lax.broadcasted_iota(jnp.int32, s.shape, 1)
        return jnp.where(kpos <= qpos, s, NEG)

    def accumulate(h, kt, s):
        ks = pl.multiple_of(kt * tk, tk)
        m_old = m_sc[h]
        m_new = jnp.maximum(m_old, jnp.max(s, axis=0, keepdims=True))
        p = jnp.exp2(s - m_new).astype(jnp.bfloat16)
        vt = jnp.concatenate(
            [vt_ref[0, h * HEAD_DIM:(h + 1) * HEAD_DIM, pl.ds(ks, tk)], ones_rows], axis=0)
        acc_sc[h] = jnp.exp2(m_old - m_new) * acc_sc[h] + _dot(vt, p)
        m_sc[h] = m_new

    def step(kt, last):
        todo = [(h, kt) for h in range(1, N_HEADS)] + ([] if last else [(0, kt + 1)])
        ready = [s_sc[...]]
        for h in range(N_HEADS):
            while todo and len(ready) < 1 + SCORE_LOOKAHEAD:
                ready.append(scores(*todo.pop(0)))
            s_cur = ready.pop(0)
            accumulate(h, kt, causal(s_cur) if last else s_cur)
        if not last:
            s_sc[...] = ready.pop(0)

    def body(kt, carry):
        step(kt, False)
        return carry

    s_sc[...] = scores(0, 0)
    lax.fori_loop(0, qi, body, 0)
    step(qi, True)
    outs = []
    for h in range(N_HEADS):
        acc = acc_sc[h]
        outs.append(acc[:HEAD_DIM] / acc[HEAD_DIM:HEAD_DIM + 1])
    o = jnp.concatenate(outs, axis=0).T
    o_ref[0] = (o * gb_ref[0].astype(jnp.float32)).astype(o_ref.dtype)


def _fox_attn(q_aug, k_aug, vt, gb):
    bsz, nh, seq, aug = q_aug.shape
    tq = T_ATT
    return pl.pallas_call(
        _fox_attn_kernel,
        grid=(bsz, seq // tq),
        in_specs=[
            pl.BlockSpec((1, nh, tq, aug), lambda b, i: (b, 0, i, 0)),
            pl.BlockSpec((1, nh, seq, aug), lambda b, i: (b, 0, 0, 0)),
            pl.BlockSpec((1, BR, seq), lambda b, i: (b, 0, 0)),
            pl.BlockSpec((1, tq, BR), lambda b, i: (b, i, 0)),
        ],
        out_specs=pl.BlockSpec((1, tq, BR), lambda b, i: (b, i, 0)),
        out_shape=jax.ShapeDtypeStruct((bsz, seq, BR), jnp.bfloat16),
        scratch_shapes=[
            pltpu.VMEM((nh, 1, tq), jnp.float32),
            pltpu.VMEM((nh, V_ROWS, tq), jnp.float32),
            pltpu.VMEM((tq, tq), jnp.float32),
        ],
        compiler_params=pltpu.CompilerParams(
            dimension_semantics=("arbitrary", "arbitrary"), vmem_limit_bytes=VMEM_LIMIT),
        name="fox_attn",
    )(q_aug, k_aug, vt, gb)


def _merge_out_kernel(x_ref, g_ref, ya_ref, yb_ref, yc_ref, yd_ref,
                      wm0_ref, wm1_ref, wb_ref, wo_ref, fg_ref, o_ref, *, final):
    x = x_ref[0]
    hb = (_rms_scale(x) * g_ref[...]).astype(jnp.bfloat16)
    ys = (ya_ref[0], yb_ref[0], yc_ref[0], yd_ref[0])
    per_ref = H_BLOCK // D_MODEL
    merged = jnp.zeros(x.shape, jnp.float32)
    for n in range(N_BRANCH):
        wm_ref = (wm0_ref, wm1_ref)[n // per_ref]
        c0 = (n % per_ref) * D_MODEL
        gate = _sigmoid(_dot(hb, wm_ref[0, :, c0:c0 + D_MODEL]))
        merged = merged + gate * _dot(ys[n], wb_ref[n])
    out = x + _dot(merged.astype(jnp.bfloat16), wo_ref[...])
    if final:
        out = _rms_scale(out) * fg_ref[...]
    o_ref[0] = out


def _merge_out(layer, x, g, ya, yb, yc, yd, w_hi, wb, wo, fg, final):
    bsz, seq, d = x.shape
    t = T_OUT
    const = lambda shape: pl.BlockSpec(shape, lambda b, j: (0,) * len(shape))
    tok = lambda width: pl.BlockSpec((1, t, width), lambda b, j: (b, j, 0))
    first = H_MERGE // H_BLOCK
    return pl.pallas_call(
        functools.partial(_merge_out_kernel, final=final),
        grid=(bsz, seq // t),
        in_specs=[
            tok(d), const(g.shape), tok(BR), tok(BR), tok(BR), tok(BR),
            pl.BlockSpec((1, d, H_BLOCK), lambda b, j: (layer, 0, first)),
            pl.BlockSpec((1, d, H_BLOCK), lambda b, j: (layer, 0, first + 1)),
            const(wb.shape), const(wo.shape), const(fg.shape),
        ],
        out_specs=tok(d),
        out_shape=jax.ShapeDtypeStruct((bsz, seq, d), jnp.float32),
        compiler_params=pltpu.CompilerParams(
            dimension_semantics=("arbitrary", "arbitrary"), vmem_limit_bytes=VMEM_LIMIT),
        name="merge_out_final" if final else "merge_out",
    )(x, g, ya, yb, yc, yd, w_hi, w_hi, wb, wo, fg)


def _head_lanes(h):
    return (0, HEAD_DIM) if h % 2 == 0 else (HEAD_DIM, SUBLANES)


def _weight_prep_kernel(w_ref, wa_ref, wqk_ref, wvt_ref, whi_ref):
    x = w_ref[0]
    bf = jnp.bfloat16
    wa_ref[0] = x[:, :W_LO].astype(bf)
    whi_ref[0] = x[:, W_HI:].astype(bf)
    lane = lax.broadcasted_iota(jnp.int32, (x.shape[0], AUG), 1)
    f_col = x[:, W_LO + 3 * BR:W_LO + 3 * BR + LANES]
    for h in range(N_HEADS):
        lo, _ = _head_lanes(h)
        in_head = (lane >= lo) & (lane < lo + HEAD_DIM)
        c0 = W_LO + (h // 2) * LANES
        q_col = x[:, c0:c0 + LANES] * (LOG2E / math.sqrt(HEAD_DIM))
        k_col = x[:, c0 + BR:c0 + BR + LANES]
        rest = jnp.where(lane < F_LANE + N_HEADS, f_col, 0.0) if h == F_HEAD else 0.0
        wqk_ref[0, :, h * AUG:(h + 1) * AUG] = jnp.where(in_head, q_col, rest).astype(bf)
        wqk_ref[0, :, (N_HEADS + h) * AUG:(N_HEADS + h + 1) * AUG] = (
            jnp.where(in_head, k_col, 0.0).astype(bf))
    wvt_ref[0] = x[:, W_LO + 2 * BR:W_LO + 3 * BR].T.astype(bf)


def _weight_prep(w_in):
    depth, d, cols = w_in.shape
    rows = PREP_ROWS
    bf = jnp.bfloat16
    return pl.pallas_call(
        _weight_prep_kernel,
        grid=(depth, d // rows),
        in_specs=[pl.BlockSpec((1, rows, cols), lambda l, i: (l, i, 0))],
        out_specs=[pl.BlockSpec((1, rows, W_LO), lambda l, i: (l, i, 0)),
                   pl.BlockSpec((1, rows, 2 * N_HEADS * AUG), lambda l, i: (l, i, 0)),
                   pl.BlockSpec((1, BR, rows), lambda l, i: (l, 0, i)),
                   pl.BlockSpec((1, rows, cols - W_HI), lambda l, i: (l, i, 0))],
        out_shape=[jax.ShapeDtypeStruct((depth, d, W_LO), bf),
                   jax.ShapeDtypeStruct((depth, d, 2 * N_HEADS * AUG), bf),
                   jax.ShapeDtypeStruct((depth, BR, d), bf),
                   jax.ShapeDtypeStruct((depth, d, cols - W_HI), bf)],
        compiler_params=pltpu.CompilerParams(
            dimension_semantics=("arbitrary", "arbitrary"), vmem_limit_bytes=VMEM_LIMIT),
        name="weight_prep",
    )(w_in)


def kernel(x, norm_g, w_in, f_bias, sgu_w, sgu_b, sgu_ln_g, sgu_ln_b, short_conv_w,
           conf_dw_w, conf_dw_b, conf_ln_g, conf_ln_b, w_branch, w_out, final_g):
    depth = norm_g.shape[0]
    bf = jnp.bfloat16
    fg = final_g.reshape(1, D_MODEL)
    w_bf, wqk, wvt, w_hi = _weight_prep(w_in)
    wb_bf = w_branch.astype(bf)
    wo_bf = w_out.astype(bf)
    for layer in range(depth):
        g = norm_g[layer].reshape(1, D_MODEL)
        fb = jnp.zeros((1, LANES), jnp.float32).at[0, F_LANE:F_LANE + N_HEADS].set(f_bias[layer])
        sguw = sgu_w[layer].reshape(N_HEADS * CHUNK, CHUNK)
        sgub = jnp.repeat(sgu_b[layer].T, HEAD_DIM, axis=1)
        aln = jnp.stack([sgu_ln_g[layer], sgu_ln_b[layer]])
        scw = jnp.zeros((SUBLANES, BR), jnp.float32).at[:SHORT_CONV].set(short_conv_w[layer])
        dww = jnp.broadcast_to(conf_dw_w[layer][:, None, :], (CONF_CONV, SUBLANES, BR))
        dvec = jnp.stack([conf_dw_b[layer], conf_ln_g[layer], conf_ln_b[layer]])
        ya, yc, yd, gb, q_aug, k_aug, vt = _mixer_in(
            layer, x, g, w_bf, wqk, w_hi, wvt, fb, sguw, sgub, aln, scw, dww, dvec)
        yb = _fox_attn(q_aug, k_aug, vt, gb)
        x = _merge_out(layer, x, g, ya, yb, yc, yd, w_hi, wb_bf[layer], wo_bf[layer], fg,
                       final=(layer == depth - 1))
    return x
```

```python
import functools
import math

import jax
import jax.numpy as jnp
import numpy as np
from jax import lax
from jax.experimental import pallas as pl
from jax.experimental.pallas import tpu as pltpu

D_MODEL = 1024
N_BRANCH = 4
BR = D_MODEL // N_BRANCH
HEAD_DIM = 64
N_HEADS = BR // HEAD_DIM
CHUNK = 128
SHORT_CONV = 3
CONF_CONV = 31
EPS = 1e-6

LANES = 128
SUBLANES = 8
AUG = LANES
CONF_HALO = 32
SHORT_HALO = SUBLANES
CONV_ROWS = 64

T_MIX = 1024
T_ATT = 512
V_ROWS = HEAD_DIM + 16
LOG2E = math.log2(math.e)
SCORE_LOOKAHEAD = 2
T_OUT = 1024
PREP_ROWS = 128
VMEM_LIMIT = 56 * 1024 * 1024

W_LO = 768
W_HI = 1540
H_BG, H_C, H_CG, H_D, H_DG, H_MERGE = 0, 256, 1024, 1280, 1792, 2048
H_BLOCK = 2048
F_HEAD, F_LANE = 1, 0
ONES_LANE = F_LANE + 3 * N_HEADS

NEG = -1e30

_NT = (((1,), (1,)), ((), ()))


def _dot(a, b):
    return jnp.dot(a, b, preferred_element_type=jnp.float32)


def _sigmoid(x):
    return 0.5 + 0.5 * jnp.tanh(0.5 * x)


def _silu(x):
    h = 0.5 * x
    return h + h * jnp.tanh(h)


def _gelu_tanh(x):
    c = math.sqrt(2.0 / math.pi)
    return 0.5 * x * (1.0 + jnp.tanh(c * (x + 0.044715 * (x * x * x))))


def _log_sigmoid(x):
    return jnp.minimum(x, 0.0) - jnp.log(1.0 + jnp.exp(-jnp.abs(x)))


def _layernorm(x, g, b):
    mu = jnp.mean(x, axis=-1, keepdims=True)
    xc = x - mu
    var = jnp.mean(xc * xc, axis=-1, keepdims=True)
    return xc * lax.rsqrt(var + EPS) * g + b


def _rms_scale(x):
    return x * lax.rsqrt(jnp.mean(x * x, axis=-1, keepdims=True) + EPS)


def _split3(x):
    hi = x.astype(jnp.bfloat16).astype(jnp.float32)
    r = x - hi
    mid = r.astype(jnp.bfloat16).astype(jnp.float32)
    lo = r - mid
    return hi, mid, lo


def _head_lanes(h):
    return (0, HEAD_DIM) if h % 2 == 0 else (HEAD_DIM, SUBLANES)


def _mixer_in_kernel(x_ref, g_ref, wa_ref, wqk_ref, wh_ref, wvt_ref, fb_ref, sguw_ref, sgub_ref,
                     aln_ref, scw_ref, dww_ref, dvec_ref, place_ref,
                     ya_ref, yc_ref, yd_ref, gb_ref, q_ref, k_ref, vt_ref,
                     zbuf, hbuf, cum_carry):
    t = x_ref.shape[1]
    j = pl.program_id(1)

    @pl.when(j == 0)
    def _():
        zbuf[0:SHORT_HALO, :] = jnp.zeros((SHORT_HALO, BR), jnp.float32)
        hbuf[0:CONF_HALO, :] = jnp.zeros((CONF_HALO, BR), jnp.float32)
        cum_carry[...] = jnp.zeros_like(cum_carry)

    x = x_ref[0]
    hb = (_rms_scale(x) * g_ref[...]).astype(jnp.bfloat16)


    glu = _dot(hb, wh_ref[0, :, H_D:H_DG])
    gd_lin = _dot(hb, wh_ref[0, :, H_DG:H_MERGE])
    hbuf[CONF_HALO:CONF_HALO + t, :] = glu[:, :BR] * _sigmoid(glu[:, BR:])

    def conf_rows(r0):
        dconv = jnp.broadcast_to(dvec_ref[0:1, :], (CONV_ROWS, BR))
        win = hbuf[r0:r0 + CONF_HALO + CONV_ROWS, :]
        for b in range(SUBLANES):
            wb = win if b == 0 else pltpu.roll(win, b, 0)
            for a in range(CONF_HALO // SUBLANES):
                s = SUBLANES * a + b
                if s < CONF_CONV:
                    kk = CONF_CONV - 1 - s
                    off = CONF_HALO - SUBLANES * a
                    wk = jnp.concatenate([dww_ref[kk]] * (CONV_ROWS // SUBLANES), axis=0)
                    dconv = dconv + wb[off:off + CONV_ROWS, :] * wk
        yd_ref[0, r0:r0 + CONV_ROWS, :] = (
            _silu(_layernorm(dconv, dvec_ref[1:2, :], dvec_ref[2:3, :]))
            * _silu(gd_lin[r0:r0 + CONV_ROWS])).astype(yd_ref.dtype)

    qa = _dot(hb, wqk_ref[0, :, 0:N_HEADS * AUG])
    ka = _dot(hb, wqk_ref[0, :, N_HEADS * AUG:])
    for r0 in range(0, t // 2, CONV_ROWS):
        conf_rows(r0)

    lf = _log_sigmoid(qa[:, F_HEAD * AUG:(F_HEAD + 1) * AUG] + fb_ref[...])
    r2 = lax.broadcasted_iota(jnp.int32, (CHUNK, CHUNK), 0)
    c2 = lax.broadcasted_iota(jnp.int32, (CHUNK, CHUNK), 1)
    tri = jnp.where(r2 >= c2, 1.0, 0.0).astype(jnp.bfloat16)
    lane_c = lax.broadcasted_iota(jnp.int32, (CHUNK, LANES), 1)
    grp = (lane_c - F_LANE) // N_HEADS
    within = []
    for c in range(t // CHUNK):
        hi, mid, lo = _split3(lf[c * CHUNK:(c + 1) * CHUNK])
        packed = jnp.where(grp == 0, hi,
                           jnp.where(grp == 1, pltpu.roll(mid, N_HEADS, 1),
                                     jnp.where(grp == 2, pltpu.roll(lo, 2 * N_HEADS, 1), 0.0)))
        r = _dot(tri, packed.astype(jnp.bfloat16))
        within.append(r + pltpu.roll(r, LANES - N_HEADS, 1)
                      + pltpu.roll(r, LANES - 2 * N_HEADS, 1))

    uv = _dot(hb, wa_ref[0, :, 0:2 * BR])
    ga_lin = _dot(hb, wa_ref[0, :, 2 * BR:W_LO])
    for r0 in range(t // 2, t, CONV_ROWS):
        conf_rows(r0)
    hbuf[0:CONF_HALO, :] = hbuf[t:t + CONF_HALO, :]

    cin = _dot(hb, wh_ref[0, :, H_C:H_CG])
    gc_lin = _dot(hb, wh_ref[0, :, H_CG:H_D])

    carry = cum_carry[...]
    cums = []
    for cc in within:
        cums.append(cc + carry)
        carry = carry + cc[CHUNK - 1:CHUNK, :]
    cum_carry[...] = carry
    cum = jnp.concatenate(cums, axis=0) * LOG2E

    hi, mid, lo = _split3(cum)
    lane_t = lax.broadcasted_iota(jnp.int32, (t, LANES), 1)
    grp_t = (lane_t - F_LANE) // N_HEADS
    packed = jnp.where(grp_t == 0, hi,
                       jnp.where(grp_t == 1, pltpu.roll(mid, N_HEADS, 1),
                                 jnp.where(grp_t == 2, pltpu.roll(lo, 2 * N_HEADS, 1),
                                           jnp.where(lane_t == ONES_LANE, 1.0, 0.0))))
    bias = _dot(packed.astype(jnp.bfloat16), place_ref[...])
    qa = qa + bias[:, :N_HEADS * AUG]
    ka = ka + bias[:, N_HEADS * AUG:]
    for h in range(N_HEADS):
        q_ref[0, h] = qa[:, h * AUG:(h + 1) * AUG].astype(q_ref.dtype)
        k_ref[0, h] = ka[:, h * AUG:(h + 1) * AUG].astype(k_ref.dtype)

    gb_lin = _dot(hb, wh_ref[0, :, H_BG:H_C])
    vt = lax.dot_general(wvt_ref[0], hb, _NT, preferred_element_type=jnp.float32)

    gel = _gelu_tanh(uv)
    u = gel[:, :BR]
    v = _layernorm(gel[:, BR:], aln_ref[0:1, :], aln_ref[1:2, :])
    rows = lax.broadcasted_iota(jnp.int32, (N_HEADS * CHUNK, CHUNK), 0)
    cols = lax.broadcasted_iota(jnp.int32, (N_HEADS * CHUNK, CHUNK), 1)
    wst = jnp.where((rows % CHUNK) >= cols, sguw_ref[...], 0.0).astype(jnp.bfloat16)
    malls = [_dot(wst, v[c * CHUNK:(c + 1) * CHUNK].astype(jnp.bfloat16))
             for c in range(t // CHUNK)]

    zbuf[SHORT_HALO:SHORT_HALO + t, :] = cin[:, BR:2 * BR] * cin[:, 2 * BR:]
    conv = jnp.zeros((t, BR), jnp.float32)
    for kk in range(SHORT_CONV):
        s = SHORT_CONV - 1 - kk
        conv = conv + zbuf[SHORT_HALO - s:SHORT_HALO - s + t, :] * scw_ref[kk:kk + 1, :]
    yc_ref[0] = (cin[:, :BR] * conv * _silu(gc_lin)).astype(yc_ref.dtype)
    zbuf[0:SHORT_HALO, :] = zbuf[t:t + SHORT_HALO, :]

    gb_ref[0] = _silu(gb_lin).astype(gb_ref.dtype)
    vt_ref[0] = vt.astype(vt_ref.dtype)

    lane_head = lax.broadcasted_iota(jnp.int32, (CHUNK, BR), 1) // HEAD_DIM
    ga = _silu(ga_lin)
    for c in range(t // CHUNK):
        sl = slice(c * CHUNK, (c + 1) * CHUNK)
        mixed = sgub_ref[...]
        for h in range(N_HEADS):
            mixed = mixed + jnp.where(lane_head == h, malls[c][h * CHUNK:(h + 1) * CHUNK], 0.0)
        ya_ref[0, sl, :] = (u[sl] * mixed * ga[sl]).astype(ya_ref.dtype)


def _mixer_in(layer, x, g, w_bf, wqk, w_hi, wvt, fb, sguw, sgub, aln, scw, dww, dvec, place):
    bsz, seq, d = x.shape
    t = T_MIX
    nblk = seq // t
    bf = jnp.bfloat16
    const = lambda shape: pl.BlockSpec(shape, lambda b, j: (0,) * len(shape))
    tok = lambda width: pl.BlockSpec((1, t, width), lambda b, j: (b, j, 0))
    return pl.pallas_call(
        _mixer_in_kernel,
        grid=(bsz, nblk),
        in_specs=[
            tok(d), const(g.shape),
            pl.BlockSpec((1, d, W_LO), lambda b, j: (layer, 0, 0)),
            pl.BlockSpec((1, d, 2 * N_HEADS * AUG), lambda b, j: (layer, 0, 0)),
            pl.BlockSpec((1, d, H_BLOCK), lambda b, j: (layer, 0, 0)),
            pl.BlockSpec((1, BR, d), lambda b, j: (layer, 0, 0)), const(fb.shape),
            const(sguw.shape), const(sgub.shape), const(aln.shape), const(scw.shape),
            const(dww.shape), const(dvec.shape), const(place.shape),
        ],
        out_specs=[
            tok(BR), tok(BR), tok(BR), tok(BR),
            pl.BlockSpec((1, N_HEADS, t, AUG), lambda b, j: (b, 0, j, 0)),
            pl.BlockSpec((1, N_HEADS, t, AUG), lambda b, j: (b, 0, j, 0)),
            pl.BlockSpec((1, BR, t), lambda b, j: (b, 0, j)),
        ],
        out_shape=[
            jax.ShapeDtypeStruct((bsz, seq, BR), bf),
            jax.ShapeDtypeStruct((bsz, seq, BR), bf),
            jax.ShapeDtypeStruct((bsz, seq, BR), bf),
            jax.ShapeDtypeStruct((bsz, seq, BR), bf),
            jax.ShapeDtypeStruct((bsz, N_HEADS, seq, AUG), bf),
            jax.ShapeDtypeStruct((bsz, N_HEADS, seq, AUG), bf),
            jax.ShapeDtypeStruct((bsz, BR, seq), bf),
        ],
        scratch_shapes=[
            pltpu.VMEM((SHORT_HALO + t, BR), jnp.float32),
            pltpu.VMEM((CONF_HALO + t, BR), jnp.float32),
            pltpu.VMEM((1, LANES), jnp.float32),
        ],
        compiler_params=pltpu.CompilerParams(
            dimension_semantics=("arbitrary", "arbitrary"), vmem_limit_bytes=VMEM_LIMIT),
        name="mixer_in",
    )(x, g, w_bf, wqk, w_hi, wvt, fb, sguw, sgub, aln, scw, dww, dvec, place)


def _fox_attn_kernel(q_ref, k_ref, vt_ref, gb_ref, o_ref, m_sc, acc_sc, s_sc):
    tq = q_ref.shape[2]
    tk = tq
    qi = pl.program_id(1)
    m_sc[...] = jnp.full(m_sc.shape, NEG, jnp.float32)
    acc_sc[...] = jnp.zeros(acc_sc.shape, jnp.float32)
    ones_rows = jnp.ones((V_ROWS - HEAD_DIM, tk), jnp.bfloat16)

    def scores(h, kt):
        ks = pl.multiple_of(kt * tk, tk)
        k = k_ref[0, h, pl.ds(ks, tk), :]
        return lax.dot_general(k, q_ref[0, h], _NT,
                               preferred_element_type=jnp.float32)

    def causal(s):
        kpos = lax.broadcasted_iota(jnp.int32, s.shape, 0)
        qpos = lax.broadcasted_iota(jnp.int32, s.shape, 1)
        return jnp.where(kpos <= qpos, s, NEG)

    def accumulate(h, kt, s):
        ks = pl.multiple_of(kt * tk, tk)
        m_old = m_sc[h]
        m_new = jnp.maximum(m_old, jnp.max(s, axis=0, keepdims=True))
        p = jnp.exp2(s - m_new).astype(jnp.bfloat16)
        vt = jnp.concatenate(
            [vt_ref[0, h * HEAD_DIM:(h + 1) * HEAD_DIM, pl.ds(ks, tk)], ones_rows], axis=0)
        acc_sc[h] = jnp.exp2(m_old - m_new) * acc_sc[h] + _dot(vt, p)
        m_sc[h] = m_new

    def step(kt, last):
        todo = [(h, kt) for h in range(1, N_HEADS)] + ([] if last else [(0, kt + 1)])
        ready = [s_sc[...]]
        for h in range(N_HEADS):
            while todo and len(ready) < 1 + SCORE_LOOKAHEAD:
                ready.append(scores(*todo.pop(0)))
            s_cur = ready.pop(0)
            accumulate(h, kt, causal(s_cur) if last else s_cur)
        if not last:
            s_sc[...] = ready.pop(0)

    def body(kt, carry):
        step(kt, False)
        return carry

    s_sc[...] = scores(0, 0)
    lax.fori_loop(0, qi, body, 0)
    step(qi, True)
    outs = []
    for h in range(N_HEADS):
        acc = acc_sc[h]
        outs.append(acc[:HEAD_DIM] / acc[HEAD_DIM:HEAD_DIM + 1])
    o = jnp.concatenate(outs, axis=0).T
    o_ref[0] = (o * gb_ref[0].astype(jnp.float32)).astype(o_ref.dtype)


def _fox_attn(q_aug, k_aug, vt, gb):
    bsz, nh, seq, aug = q_aug.shape
    tq = T_ATT
    return pl.pallas_call(
        _fox_attn_kernel,
        grid=(bsz, seq // tq),
        in_specs=[
            pl.BlockSpec((1, nh, tq, aug), lambda b, i: (b, 0, i, 0)),
            pl.BlockSpec((1, nh, seq, aug), lambda b, i: (b, 0, 0, 0)),
            pl.BlockSpec((1, BR, seq), lambda b, i: (b, 0, 0)),
            pl.BlockSpec((1, tq, BR), lambda b, i: (b, i, 0)),
        ],
        out_specs=pl.BlockSpec((1, tq, BR), lambda b, i: (b, i, 0)),
        out_shape=jax.ShapeDtypeStruct((bsz, seq, BR), jnp.bfloat16),
        scratch_shapes=[
            pltpu.VMEM((nh, 1, tq), jnp.float32),
            pltpu.VMEM((nh, V_ROWS, tq), jnp.float32),
            pltpu.VMEM((tq, tq), jnp.float32),
        ],
        compiler_params=pltpu.CompilerParams(
            dimension_semantics=("arbitrary", "arbitrary"), vmem_limit_bytes=VMEM_LIMIT),
        name="fox_attn",
    )(q_aug, k_aug, vt, gb)


def _merge_out_kernel(x_ref, g_ref, ya_ref, yb_ref, yc_ref, yd_ref,
                      wm0_ref, wm1_ref, wb_ref, wo_ref, fg_ref, o_ref, *, final):
    x = x_ref[0]
    hb = (_rms_scale(x) * g_ref[...]).astype(jnp.bfloat16)
    ys = (ya_ref[0], yb_ref[0], yc_ref[0], yd_ref[0])
    per_ref = H_BLOCK // D_MODEL
    merged = jnp.zeros(x.shape, jnp.float32)
    for n in range(N_BRANCH):
        wm_ref = (wm0_ref, wm1_ref)[n // per_ref]
        c0 = (n % per_ref) * D_MODEL
        gate = _sigmoid(_dot(hb, wm_ref[0, :, c0:c0 + D_MODEL]))
        merged = merged + gate * _dot(ys[n], wb_ref[n])
    out = x + _dot(merged.astype(jnp.bfloat16), wo_ref[...])
    if final:
        out = _rms_scale(out) * fg_ref[...]
    o_ref[0] = out


def _merge_out(layer, x, g, ya, yb, yc, yd, w_hi, wb, wo, fg, final):
    bsz, seq, d = x.shape
    t = T_OUT
    const = lambda shape: pl.BlockSpec(shape, lambda b, j: (0,) * len(shape))
    tok = lambda width: pl.BlockSpec((1, t, width), lambda b, j: (b, j, 0))
    first = H_MERGE // H_BLOCK
    return pl.pallas_call(
        functools.partial(_merge_out_kernel, final=final),
        grid=(bsz, seq // t),
        in_specs=[
            tok(d), const(g.shape), tok(BR), tok(BR), tok(BR), tok(BR),
            pl.BlockSpec((1, d, H_BLOCK), lambda b, j: (layer, 0, first)),
            pl.BlockSpec((1, d, H_BLOCK), lambda b, j: (layer, 0, first + 1)),
            const(wb.shape), const(wo.shape), const(fg.shape),
        ],
        out_specs=tok(d),
        out_shape=jax.ShapeDtypeStruct((bsz, seq, d), jnp.float32),
        compiler_params=pltpu.CompilerParams(
            dimension_semantics=("arbitrary", "arbitrary"), vmem_limit_bytes=VMEM_LIMIT),
        name="merge_out_final" if final else "merge_out",
    )(x, g, ya, yb, yc, yd, w_hi, w_hi, wb, wo, fg)


def _weight_prep_kernel(w_ref, wa_ref, wqk_ref, wvt_ref, whi_ref):
    x = w_ref[0]
    bf = jnp.bfloat16
    wa_ref[0] = x[:, :W_LO].astype(bf)
    whi_ref[0] = x[:, W_HI:].astype(bf)
    lane = lax.broadcasted_iota(jnp.int32, (x.shape[0], AUG), 1)
    f_col = x[:, W_LO + 3 * BR:W_LO + 3 * BR + LANES]
    for h in range(N_HEADS):
        lo, _ = _head_lanes(h)
        in_head = (lane >= lo) & (lane < lo + HEAD_DIM)
        c0 = W_LO + (h // 2) * LANES
        q_col = x[:, c0:c0 + LANES] * (LOG2E / math.sqrt(HEAD_DIM))
        k_col = x[:, c0 + BR:c0 + BR + LANES]
        rest = jnp.where(lane < F_LANE + N_HEADS, f_col, 0.0) if h == F_HEAD else 0.0
        wqk_ref[0, :, h * AUG:(h + 1) * AUG] = jnp.where(in_head, q_col, rest).astype(bf)
        wqk_ref[0, :, (N_HEADS + h) * AUG:(N_HEADS + h + 1) * AUG] = (
            jnp.where(in_head, k_col, 0.0).astype(bf))
    wvt_ref[0] = x[:, W_LO + 2 * BR:W_LO + 3 * BR].T.astype(bf)


def _weight_prep(w_in):
    depth, d, cols = w_in.shape
    rows = PREP_ROWS
    bf = jnp.bfloat16
    return pl.pallas_call(
        _weight_prep_kernel,
        grid=(depth, d // rows),
        in_specs=[pl.BlockSpec((1, rows, cols), lambda l, i: (l, i, 0))],
        out_specs=[pl.BlockSpec((1, rows, W_LO), lambda l, i: (l, i, 0)),
                   pl.BlockSpec((1, rows, 2 * N_HEADS * AUG), lambda l, i: (l, i, 0)),
                   pl.BlockSpec((1, BR, rows), lambda l, i: (l, 0, i)),
                   pl.BlockSpec((1, rows, cols - W_HI), lambda l, i: (l, i, 0))],
        out_shape=[jax.ShapeDtypeStruct((depth, d, W_LO), bf),
                   jax.ShapeDtypeStruct((depth, d, 2 * N_HEADS * AUG), bf),
                   jax.ShapeDtypeStruct((depth, BR, d), bf),
                   jax.ShapeDtypeStruct((depth, d, cols - W_HI), bf)],
        compiler_params=pltpu.CompilerParams(
            dimension_semantics=("arbitrary", "arbitrary"), vmem_limit_bytes=VMEM_LIMIT),
        name="weight_prep",
    )(w_in)


def _bias_placement():
    place = np.zeros((LANES, 2 * N_HEADS * AUG), np.float32)
    for h in range(N_HEADS):
        _, a0 = _head_lanes(h)
        qc, kc = h * AUG + a0, (N_HEADS + h) * AUG + a0
        for piece in range(3):
            src = F_LANE + piece * N_HEADS + h
            place[src, qc + piece] = 1.0
            place[src, kc + 3 + piece] = -1.0
            place[ONES_LANE, qc + 3 + piece] = 1.0
            place[ONES_LANE, kc + piece] = 1.0
    return jnp.asarray(place, jnp.bfloat16)


def kernel(x, norm_g, w_in, f_bias, sgu_w, sgu_b, sgu_ln_g, sgu_ln_b, short_conv_w,
           conf_dw_w, conf_dw_b, conf_ln_g, conf_ln_b, w_branch, w_out, final_g):
    depth = norm_g.shape[0]
    bf = jnp.bfloat16
    fg = final_g.reshape(1, D_MODEL)
    w_bf, wqk, wvt, w_hi = _weight_prep(w_in)
    wb_bf = w_branch.astype(bf)
    wo_bf = w_out.astype(bf)
    place = _bias_placement()
    for layer in range(depth):
        g = norm_g[layer].reshape(1, D_MODEL)
        fb = jnp.zeros((1, LANES), jnp.float32).at[0, F_LANE:F_LANE + N_HEADS].set(f_bias[layer])
        sguw = sgu_w[layer].reshape(N_HEADS * CHUNK, CHUNK)
        sgub = jnp.repeat(sgu_b[layer].T, HEAD_DIM, axis=1)
        aln = jnp.stack([sgu_ln_g[layer], sgu_ln_b[layer]])
        scw = jnp.zeros((SUBLANES, BR), jnp.float32).at[:SHORT_CONV].set(short_conv_w[layer])
        dww = jnp.broadcast_to(conf_dw_w[layer][:, None, :], (CONF_CONV, SUBLANES, BR))
        dvec = jnp.stack([conf_dw_b[layer], conf_ln_g[layer], conf_ln_b[layer]])
        ya, yc, yd, gb, q_aug, k_aug, vt = _mixer_in(
            layer, x, g, w_bf, wqk, w_hi, wvt, fb, sguw, sgub, aln, scw, dww, dvec, place)
        yb = _fox_attn(q_aug, k_aug, vt, gb)
        x = _merge_out(layer, x, g, ya, yb, yc, yd, w_hi, wb_bf[layer], wo_bf[layer], fg,
                       final=(layer == depth - 1))
    return x
```

```python
import functools
import math

import jax
import jax.numpy as jnp
import numpy as np
from jax import lax
from jax.experimental import pallas as pl
from jax.experimental.pallas import tpu as pltpu

D_MODEL = 1024
N_BRANCH = 4
BR = D_MODEL // N_BRANCH
HEAD_DIM = 64
N_HEADS = BR // HEAD_DIM
CHUNK = 128
SHORT_CONV = 3
CONF_CONV = 31
EPS = 1e-6

LANES = 128
SUBLANES = 8
AUG = LANES
CONF_HALO = 32
SHORT_HALO = SUBLANES
CONV_ROWS = 64

T_MIX = 1024
T_ATT = 512
V_ROWS = HEAD_DIM + 16
LOG2E = math.log2(math.e)
SCORE_LOOKAHEAD = 2
T_OUT = 1024
PREP_COLS = 512
QKVF_COLS = 896
VMEM_LIMIT = 56 * 1024 * 1024

W_LO = 768
W_HI = 1540
H_BG, H_C, H_CG, H_D, H_DG, H_MERGE = 0, 256, 1024, 1280, 1792, 2048
H_BLOCK = 2048
F_HEAD, F_LANE = 1, 0
ONES_LANE = F_LANE + 3 * N_HEADS

NEG = -1e30

_NT = (((1,), (1,)), ((), ()))


def _dot(a, b):
    return jnp.dot(a, b, preferred_element_type=jnp.float32)


def _sigmoid(x):
    return 0.5 + 0.5 * jnp.tanh(0.5 * x)


def _silu(x):
    h = 0.5 * x
    return h + h * jnp.tanh(h)


def _gelu_tanh(x):
    c = math.sqrt(2.0 / math.pi)
    return 0.5 * x * (1.0 + jnp.tanh(c * (x + 0.044715 * (x * x * x))))


def _log_sigmoid(x):
    return jnp.minimum(x, 0.0) - jnp.log(1.0 + jnp.exp(-jnp.abs(x)))


def _layernorm(x, g, b):
    mu = jnp.mean(x, axis=-1, keepdims=True)
    xc = x - mu
    var = jnp.mean(xc * xc, axis=-1, keepdims=True)
    return xc * lax.rsqrt(var + EPS) * g + b


def _rms_scale(x):
    return x * lax.rsqrt(jnp.mean(x * x, axis=-1, keepdims=True) + EPS)


def _split3(x):
    hi = x.astype(jnp.bfloat16).astype(jnp.float32)
    r = x - hi
    mid = r.astype(jnp.bfloat16).astype(jnp.float32)
    lo = r - mid
    return hi, mid, lo


def _head_lanes(h):
    return (0, HEAD_DIM) if h % 2 == 0 else (HEAD_DIM, SUBLANES)


def _mixer_in_kernel(x_ref, g_ref, wa_ref, wqk_ref, wh_ref, wvt_ref, fb_ref, sguw_ref, sgub_ref,
                     aln_ref, scw_ref, dww_ref, dvec_ref, place_ref,
                     ya_ref, yc_ref, yd_ref, gb_ref, q_ref, k_ref, vt_ref,
                     zbuf, hbuf, cum_carry):
    t = x_ref.shape[1]
    j = pl.program_id(1)

    @pl.when(j == 0)
    def _():
        zbuf[0:SHORT_HALO, :] = jnp.zeros((SHORT_HALO, BR), jnp.float32)
        hbuf[0:CONF_HALO, :] = jnp.zeros((CONF_HALO, BR), jnp.float32)
        cum_carry[...] = jnp.zeros_like(cum_carry)

    x = x_ref[0]
    hb = (_rms_scale(x) * g_ref[...]).astype(jnp.bfloat16)


    glu = _dot(hb, wh_ref[0, :, H_D:H_DG])
    gd_lin = _dot(hb, wh_ref[0, :, H_DG:H_MERGE])
    hbuf[CONF_HALO:CONF_HALO + t, :] = glu[:, :BR] * _sigmoid(glu[:, BR:])

    def conf_rows(r0):
        dconv = jnp.broadcast_to(dvec_ref[0:1, :], (CONV_ROWS, BR))
        win = hbuf[r0:r0 + CONF_HALO + CONV_ROWS, :]
        for b in range(SUBLANES):
            wb = win if b == 0 else pltpu.roll(win, b, 0)
            for a in range(CONF_HALO // SUBLANES):
                s = SUBLANES * a + b
                if s < CONF_CONV:
                    kk = CONF_CONV - 1 - s
                    off = CONF_HALO - SUBLANES * a
                    wk = jnp.concatenate([dww_ref[kk]] * (CONV_ROWS // SUBLANES), axis=0)
                    dconv = dconv + wb[off:off + CONV_ROWS, :] * wk
        yd_ref[0, r0:r0 + CONV_ROWS, :] = (
            _silu(_layernorm(dconv, dvec_ref[1:2, :], dvec_ref[2:3, :]))
            * _silu(gd_lin[r0:r0 + CONV_ROWS])).astype(yd_ref.dtype)

    qa = _dot(hb, wqk_ref[0, :, 0:N_HEADS * AUG])
    ka = _dot(hb, wqk_ref[0, :, N_HEADS * AUG:])
    for r0 in range(0, t // 2, CONV_ROWS):
        conf_rows(r0)

    lf = _log_sigmoid(qa[:, F_HEAD * AUG:(F_HEAD + 1) * AUG] + fb_ref[...])
    r2 = lax.broadcasted_iota(jnp.int32, (CHUNK, CHUNK), 0)
    c2 = lax.broadcasted_iota(jnp.int32, (CHUNK, CHUNK), 1)
    tri = jnp.where(r2 >= c2, 1.0, 0.0).astype(jnp.bfloat16)
    lane_c = lax.broadcasted_iota(jnp.int32, (CHUNK, LANES), 1)
    grp = (lane_c - F_LANE) // N_HEADS
    within = []
    for c in range(t // CHUNK):
        hi, mid, lo = _split3(lf[c * CHUNK:(c + 1) * CHUNK])
        packed = jnp.where(grp == 0, hi,
                           jnp.where(grp == 1, pltpu.roll(mid, N_HEADS, 1),
                                     jnp.where(grp == 2, pltpu.roll(lo, 2 * N_HEADS, 1), 0.0)))
        r = _dot(tri, packed.astype(jnp.bfloat16))
        within.append(r + pltpu.roll(r, LANES - N_HEADS, 1)
                      + pltpu.roll(r, LANES - 2 * N_HEADS, 1))

    uv = _dot(hb, wa_ref[0, :, 0:2 * BR])
    ga_lin = _dot(hb, wa_ref[0, :, 2 * BR:W_LO])
    for r0 in range(t // 2, t, CONV_ROWS):
        conf_rows(r0)
    hbuf[0:CONF_HALO, :] = hbuf[t:t + CONF_HALO, :]

    cin = _dot(hb, wh_ref[0, :, H_C:H_CG])
    gc_lin = _dot(hb, wh_ref[0, :, H_CG:H_D])

    carry = cum_carry[...]
    cums = []
    for cc in within:
        cums.append(cc + carry)
        carry = carry + cc[CHUNK - 1:CHUNK, :]
    cum_carry[...] = carry
    cum = jnp.concatenate(cums, axis=0) * LOG2E

    hi, mid, lo = _split3(cum)
    lane_t = lax.broadcasted_iota(jnp.int32, (t, LANES), 1)
    grp_t = (lane_t - F_LANE) // N_HEADS
    packed = jnp.where(grp_t == 0, hi,
                       jnp.where(grp_t == 1, pltpu.roll(mid, N_HEADS, 1),
                                 jnp.where(grp_t == 2, pltpu.roll(lo, 2 * N_HEADS, 1),
                                           jnp.where(lane_t == ONES_LANE, 1.0, 0.0))))
    bias = _dot(packed.astype(jnp.bfloat16), place_ref[...])
    qa = qa + bias[:, :N_HEADS * AUG]
    ka = ka + bias[:, N_HEADS * AUG:]
    for h in range(N_HEADS):
        q_ref[0, h] = qa[:, h * AUG:(h + 1) * AUG].astype(q_ref.dtype)
        k_ref[0, h] = ka[:, h * AUG:(h + 1) * AUG].astype(k_ref.dtype)

    gb_lin = _dot(hb, wh_ref[0, :, H_BG:H_C])
    vt = lax.dot_general(wvt_ref[0], hb, _NT, preferred_element_type=jnp.float32)

    gel = _gelu_tanh(uv)
    u = gel[:, :BR]
    v = _layernorm(gel[:, BR:], aln_ref[0:1, :], aln_ref[1:2, :])
    rows = lax.broadcasted_iota(jnp.int32, (N_HEADS * CHUNK, CHUNK), 0)
    cols = lax.broadcasted_iota(jnp.int32, (N_HEADS * CHUNK, CHUNK), 1)
    wst = jnp.where((rows % CHUNK) >= cols, sguw_ref[...], 0.0).astype(jnp.bfloat16)
    malls = [_dot(wst, v[c * CHUNK:(c + 1) * CHUNK].astype(jnp.bfloat16))
             for c in range(t // CHUNK)]

    zbuf[SHORT_HALO:SHORT_HALO + t, :] = cin[:, BR:2 * BR] * cin[:, 2 * BR:]
    conv = jnp.zeros((t, BR), jnp.float32)
    for kk in range(SHORT_CONV):
        s = SHORT_CONV - 1 - kk
        conv = conv + zbuf[SHORT_HALO - s:SHORT_HALO - s + t, :] * scw_ref[kk:kk + 1, :]
    yc_ref[0] = (cin[:, :BR] * conv * _silu(gc_lin)).astype(yc_ref.dtype)
    zbuf[0:SHORT_HALO, :] = zbuf[t:t + SHORT_HALO, :]

    gb_ref[0] = _silu(gb_lin).astype(gb_ref.dtype)
    vt_ref[0] = vt.astype(vt_ref.dtype)

    lane_head = lax.broadcasted_iota(jnp.int32, (CHUNK, BR), 1) // HEAD_DIM
    ga = _silu(ga_lin)
    for c in range(t // CHUNK):
        sl = slice(c * CHUNK, (c + 1) * CHUNK)
        mixed = sgub_ref[...]
        for h in range(N_HEADS):
            mixed = mixed + jnp.where(lane_head == h, malls[c][h * CHUNK:(h + 1) * CHUNK], 0.0)
        ya_ref[0, sl, :] = (u[sl] * mixed * ga[sl]).astype(ya_ref.dtype)


def _mixer_in(layer, x, g, w_bf, wqk, w_hi, wvt, fb, sguw, sgub, aln, scw, dww, dvec, place):
    bsz, seq, d = x.shape
    t = T_MIX
    nblk = seq // t
    bf = jnp.bfloat16
    const = lambda shape: pl.BlockSpec(shape, lambda b, j: (0,) * len(shape))
    tok = lambda width: pl.BlockSpec((1, t, width), lambda b, j: (b, j, 0))
    return pl.pallas_call(
        _mixer_in_kernel,
        grid=(bsz, nblk),
        in_specs=[
            tok(d), const(g.shape),
            pl.BlockSpec((1, d, W_LO), lambda b, j: (layer, 0, 0)),
            pl.BlockSpec((1, d, 2 * N_HEADS * AUG), lambda b, j: (layer, 0, 0)),
            pl.BlockSpec((1, d, H_BLOCK), lambda b, j: (layer, 0, 0)),
            pl.BlockSpec((1, BR, d), lambda b, j: (layer, 0, 0)), const(fb.shape),
            const(sguw.shape), const(sgub.shape), const(aln.shape), const(scw.shape),
            const(dww.shape), const(dvec.shape), const(place.shape),
        ],
        out_specs=[
            tok(BR), tok(BR), tok(BR), tok(BR),
            pl.BlockSpec((1, N_HEADS, t, AUG), lambda b, j: (b, 0, j, 0)),
            pl.BlockSpec((1, N_HEADS, t, AUG), lambda b, j: (b, 0, j, 0)),
            pl.BlockSpec((1, BR, t), lambda b, j: (b, 0, j)),
        ],
        out_shape=[
            jax.ShapeDtypeStruct((bsz, seq, BR), bf),
            jax.ShapeDtypeStruct((bsz, seq, BR), bf),
            jax.ShapeDtypeStruct((bsz, seq, BR), bf),
            jax.ShapeDtypeStruct((bsz, seq, BR), bf),
            jax.ShapeDtypeStruct((bsz, N_HEADS, seq, AUG), bf),
            jax.ShapeDtypeStruct((bsz, N_HEADS, seq, AUG), bf),
            jax.ShapeDtypeStruct((bsz, BR, seq), bf),
        ],
        scratch_shapes=[
            pltpu.VMEM((SHORT_HALO + t, BR), jnp.float32),
            pltpu.VMEM((CONF_HALO + t, BR), jnp.float32),
            pltpu.VMEM((1, LANES), jnp.float32),
        ],
        compiler_params=pltpu.CompilerParams(
            dimension_semantics=("arbitrary", "arbitrary"), vmem_limit_bytes=VMEM_LIMIT),
        name="mixer_in",
    )(x, g, w_bf, wqk, w_hi, wvt, fb, sguw, sgub, aln, scw, dww, dvec, place)


def _fox_attn_kernel(q_ref, k_ref, vt_ref, gb_ref, o_ref, m_sc, acc_sc, s_sc):
    tq = q_ref.shape[2]
    tk = tq
    qi = pl.program_id(1)
    m_sc[...] = jnp.full(m_sc.shape, NEG, jnp.float32)
    acc_sc[...] = jnp.zeros(acc_sc.shape, jnp.float32)
    ones_rows = jnp.ones((V_ROWS - HEAD_DIM, tk), jnp.bfloat16)

    def scores(h, kt):
        ks = pl.multiple_of(kt * tk, tk)
        k = k_ref[0, h, pl.ds(ks, tk), :]
        return lax.dot_general(k, q_ref[0, h], _NT,
                               preferred_element_type=jnp.float32)

    def causal(s):
        kpos = lax.broadcasted_iota(jnp.int32, s.shape, 0)
        qpos = lax.broadcasted_iota(jnp.int32, s.shape, 1)
        return jnp.where(kpos <= qpos, s, NEG)

    def accumulate(h, kt, s):
        ks = pl.multiple_of(kt * tk, tk)
        m_old = m_sc[h]
        m_new = jnp.maximum(m_old, jnp.max(s, axis=0, keepdims=True))
        p = jnp.exp2(s - m_new).astype(jnp.bfloat16)
        vt = jnp.concatenate(
            [vt_ref[0, h * HEAD_DIM:(h + 1) * HEAD_DIM, pl.ds(ks, tk)], ones_rows], axis=0)
        acc_sc[h] = jnp.exp2(m_old - m_new) * acc_sc[h] + _dot(vt, p)
        m_sc[h] = m_new

    def step(kt):
        todo = [(h, kt) for h in range(1, N_HEADS)] + [(0, kt + 1)]
        ready = [s_sc[...]]
        for h in range(N_HEADS):
            while todo and len(ready) < 1 + SCORE_LOOKAHEAD:
                ready.append(scores(*todo.pop(0)))
            accumulate(h, kt, ready.pop(0))
        s_sc[...] = ready.pop(0)

    half = tk // 2

    def diag_scores(h, ks):
        if h == 0:
            s = s_sc[...]
            return s[:half], s[half:, half:]
        k_top = k_ref[0, h, pl.ds(ks, half), :]
        k_bot = k_ref[0, h, pl.ds(pl.multiple_of(ks + half, half), half), :]
        s_top = lax.dot_general(k_top, q_ref[0, h], _NT, preferred_element_type=jnp.float32)
        s_bot = lax.dot_general(k_bot, q_ref[0, h, half:, :], _NT,
                                preferred_element_type=jnp.float32)
        return s_top, s_bot

    def diag_accumulate(h, ks, s_top, s_bot):
        s_tl, s_tr, s_br = causal(s_top[:, :half]), s_top[:, half:], causal(s_bot)
        m_old = m_sc[h]
        m_l = jnp.maximum(m_old[:, :half], jnp.max(s_tl, axis=0, keepdims=True))
        m_r = jnp.maximum(
            m_old[:, half:],
            jnp.maximum(jnp.max(s_tr, axis=0, keepdims=True), jnp.max(s_br, axis=0, keepdims=True)))
        p_tl = jnp.exp2(s_tl - m_l).astype(jnp.bfloat16)
        p_tr = jnp.exp2(s_tr - m_r).astype(jnp.bfloat16)
        p_br = jnp.exp2(s_br - m_r).astype(jnp.bfloat16)
        rows = slice(h * HEAD_DIM, (h + 1) * HEAD_DIM)
        vt_top = jnp.concatenate([vt_ref[0, rows, pl.ds(ks, half)], ones_rows[:, :half]], axis=0)
        vt_bot = jnp.concatenate(
            [vt_ref[0, rows, pl.ds(pl.multiple_of(ks + half, half), half)], ones_rows[:, :half]],
            axis=0)
        acc = acc_sc[h]
        acc_l = jnp.exp2(m_old[:, :half] - m_l) * acc[:, :half] + _dot(vt_top, p_tl)
        acc_r = (jnp.exp2(m_old[:, half:] - m_r) * acc[:, half:]
                 + _dot(vt_top, p_tr) + _dot(vt_bot, p_br))
        acc_sc[h] = jnp.concatenate([acc_l, acc_r], axis=1)

    def diag_step(kt):
        ks = pl.multiple_of(kt * tk, tk)
        todo = list(range(1, N_HEADS))
        ready = [diag_scores(0, ks)]
        for h in range(N_HEADS):
            while todo and len(ready) < 1 + SCORE_LOOKAHEAD:
                ready.append(diag_scores(todo.pop(0), ks))
            diag_accumulate(h, ks, *ready.pop(0))

    def body(kt, carry):
        step(kt)
        return carry

    s_sc[...] = scores(0, 0)
    lax.fori_loop(0, qi, body, 0)
    diag_step(qi)
    outs = []
    for h in range(N_HEADS):
        acc = acc_sc[h]
        outs.append(acc[:HEAD_DIM] / acc[HEAD_DIM:HEAD_DIM + 1])
    o = jnp.concatenate(outs, axis=0).T
    o_ref[0] = (o * gb_ref[0].astype(jnp.float32)).astype(o_ref.dtype)


def _fox_attn(q_aug, k_aug, vt, gb):
    bsz, nh, seq, aug = q_aug.shape
    tq = T_ATT
    return pl.pallas_call(
        _fox_attn_kernel,
        grid=(bsz, seq // tq),
        in_specs=[
            pl.BlockSpec((1, nh, tq, aug), lambda b, i: (b, 0, i, 0)),
            pl.BlockSpec((1, nh, seq, aug), lambda b, i: (b, 0, 0, 0)),
            pl.BlockSpec((1, BR, seq), lambda b, i: (b, 0, 0)),
            pl.BlockSpec((1, tq, BR), lambda b, i: (b, i, 0)),
        ],
        out_specs=pl.BlockSpec((1, tq, BR), lambda b, i: (b, i, 0)),
        out_shape=jax.ShapeDtypeStruct((bsz, seq, BR), jnp.bfloat16),
        scratch_shapes=[
            pltpu.VMEM((nh, 1, tq), jnp.float32),
            pltpu.VMEM((nh, V_ROWS, tq), jnp.float32),
            pltpu.VMEM((tq, tq), jnp.float32),
        ],
        compiler_params=pltpu.CompilerParams(
            dimension_semantics=("arbitrary", "arbitrary"), vmem_limit_bytes=VMEM_LIMIT),
        name="fox_attn",
    )(q_aug, k_aug, vt, gb)


def _merge_out_kernel(x_ref, g_ref, ya_ref, yb_ref, yc_ref, yd_ref,
                      wm0_ref, wm1_ref, wb_ref, wo_ref, fg_ref, o_ref, *, final):
    x = x_ref[0]
    hb = (_rms_scale(x) * g_ref[...]).astype(jnp.bfloat16)
    ys = (ya_ref[0], yb_ref[0], yc_ref[0], yd_ref[0])
    per_ref = H_BLOCK // D_MODEL
    merged = jnp.zeros(x.shape, jnp.float32)
    for n in range(N_BRANCH):
        wm_ref = (wm0_ref, wm1_ref)[n // per_ref]
        c0 = (n % per_ref) * D_MODEL
        gate = _sigmoid(_dot(hb, wm_ref[0, :, c0:c0 + D_MODEL]))
        merged = merged + gate * _dot(ys[n], wb_ref[n])
    out = x + _dot(merged.astype(jnp.bfloat16), wo_ref[...])
    if final:
        out = _rms_scale(out) * fg_ref[...]
    o_ref[0] = out


def _merge_out(layer, x, g, ya, yb, yc, yd, w_hi, wb, wo, fg, final):
    bsz, seq, d = x.shape
    t = T_OUT
    const = lambda shape: pl.BlockSpec(shape, lambda b, j: (0,) * len(shape))
    tok = lambda width: pl.BlockSpec((1, t, width), lambda b, j: (b, j, 0))
    first = H_MERGE // H_BLOCK
    return pl.pallas_call(
        functools.partial(_merge_out_kernel, final=final),
        grid=(bsz, seq // t),
        in_specs=[
            tok(d), const(g.shape), tok(BR), tok(BR), tok(BR), tok(BR),
            pl.BlockSpec((1, d, H_BLOCK), lambda b, j: (layer, 0, first)),
            pl.BlockSpec((1, d, H_BLOCK), lambda b, j: (layer, 0, first + 1)),
            const(wb.shape), const(wo.shape), const(fg.shape),
        ],
        out_specs=tok(d),
        out_shape=jax.ShapeDtypeStruct((bsz, seq, d), jnp.float32),
        compiler_params=pltpu.CompilerParams(
            dimension_semantics=("arbitrary", "arbitrary"), vmem_limit_bytes=VMEM_LIMIT),
        name="merge_out_final" if final else "merge_out",
    )(x, g, ya, yb, yc, yd, w_hi, w_hi, wb, wo, fg)


def _weight_prep_kernel(wa_in, wq_in, wh_in, wa_ref, wqk_ref, wvt_ref, whi_ref):
    bf = jnp.bfloat16
    depth = wh_in.shape[1]
    for l in range(depth):
        whi_ref[l] = wh_in[:, l, :].T.astype(bf)

    @pl.when(pl.program_id(0) == 0)
    def _():
        lane = lax.broadcasted_iota(jnp.int32, (wa_in.shape[2], AUG), 1)
        for l in range(depth):
            wa_ref[l] = wa_in[:, l, :].T.astype(bf)
            xq = wq_in[:, l, :]
            wvt_ref[l] = xq[2 * BR:3 * BR].astype(bf)
            x = xq.T
            f_col = x[:, 3 * BR:3 * BR + LANES]
            for h in range(N_HEADS):
                lo, _ = _head_lanes(h)
                in_head = (lane >= lo) & (lane < lo + HEAD_DIM)
                c0 = (h // 2) * LANES
                q_col = x[:, c0:c0 + LANES] * (LOG2E / math.sqrt(HEAD_DIM))
                k_col = x[:, c0 + BR:c0 + BR + LANES]
                rest = jnp.where(lane < F_LANE + N_HEADS, f_col, 0.0) if h == F_HEAD else 0.0
                wqk_ref[l, :, h * AUG:(h + 1) * AUG] = jnp.where(in_head, q_col, rest).astype(bf)
                wqk_ref[l, :, (N_HEADS + h) * AUG:(N_HEADS + h + 1) * AUG] = (
                    jnp.where(in_head, k_col, 0.0).astype(bf))


def _weight_prep(w_in):
    depth, d, cols = w_in.shape
    bf = jnp.bfloat16
    w_t = jnp.transpose(w_in, (2, 0, 1))
    n_hi = (cols - W_HI) // PREP_COLS
    window = lambda n: (pl.Element(n), pl.Element(depth), pl.Element(d))
    return pl.pallas_call(
        _weight_prep_kernel,
        grid=(n_hi,),
        in_specs=[
            pl.BlockSpec(window(W_LO), lambda i: (0, 0, 0)),
            pl.BlockSpec(window(QKVF_COLS), lambda i: (W_LO, 0, 0)),
            pl.BlockSpec(window(PREP_COLS), lambda i: (W_HI + PREP_COLS * i, 0, 0)),
        ],
        out_specs=[pl.BlockSpec((depth, d, W_LO), lambda i: (0, 0, 0)),
                   pl.BlockSpec((depth, d, 2 * N_HEADS * AUG), lambda i: (0, 0, 0)),
                   pl.BlockSpec((depth, BR, d), lambda i: (0, 0, 0)),
                   pl.BlockSpec((depth, d, PREP_COLS), lambda i: (0, 0, i))],
        out_shape=[jax.ShapeDtypeStruct((depth, d, W_LO), bf),
                   jax.ShapeDtypeStruct((depth, d, 2 * N_HEADS * AUG), bf),
                   jax.ShapeDtypeStruct((depth, BR, d), bf),
                   jax.ShapeDtypeStruct((depth, d, cols - W_HI), bf)],
        compiler_params=pltpu.CompilerParams(
            dimension_semantics=("arbitrary",), vmem_limit_bytes=VMEM_LIMIT),
        name="weight_prep",
    )(w_t, w_t, w_t)


def _bias_placement():
    place = np.zeros((LANES, 2 * N_HEADS * AUG), np.float32)
    for h in range(N_HEADS):
        _, a0 = _head_lanes(h)
        qc, kc = h * AUG + a0, (N_HEADS + h) * AUG + a0
        for piece in range(3):
            src = F_LANE + piece * N_HEADS + h
            place[src, qc + piece] = 1.0
            place[src, kc + 3 + piece] = -1.0
            place[ONES_LANE, qc + 3 + piece] = 1.0
            place[ONES_LANE, kc + piece] = 1.0
    return jnp.asarray(place, jnp.bfloat16)


def kernel(x, norm_g, w_in, f_bias, sgu_w, sgu_b, sgu_ln_g, sgu_ln_b, short_conv_w,
           conf_dw_w, conf_dw_b, conf_ln_g, conf_ln_b, w_branch, w_out, final_g):
    depth = norm_g.shape[0]
    bf = jnp.bfloat16
    fg = final_g.reshape(1, D_MODEL)
    w_bf, wqk, wvt, w_hi = _weight_prep(w_in)
    wb_bf = w_branch.astype(bf)
    wo_bf = w_out.astype(bf)
    place = _bias_placement()
    for layer in range(depth):
        g = norm_g[layer].reshape(1, D_MODEL)
        fb = jnp.zeros((1, LANES), jnp.float32).at[0, F_LANE:F_LANE + N_HEADS].set(f_bias[layer])
        sguw = sgu_w[layer].reshape(N_HEADS * CHUNK, CHUNK)
        sgub = jnp.repeat(sgu_b[layer].T, HEAD_DIM, axis=1)
        aln = jnp.stack([sgu_ln_g[layer], sgu_ln_b[layer]])
        scw = jnp.zeros((SUBLANES, BR), jnp.float32).at[:SHORT_CONV].set(short_conv_w[layer])
        dww = jnp.broadcast_to(conf_dw_w[layer][:, None, :], (CONF_CONV, SUBLANES, BR))
        dvec = jnp.stack([conf_dw_b[layer], conf_ln_g[layer], conf_ln_b[layer]])
        ya, yc, yd, gb, q_aug, k_aug, vt = _mixer_in(
            layer, x, g, w_bf, wqk, w_hi, wvt, fb, sguw, sgub, aln, scw, dww, dvec, place)
        yb = _fox_attn(q_aug, k_aug, vt, gb)
        x = _merge_out(layer, x, g, ya, yb, yc, yd, w_hi, wb_bf[layer], wo_bf[layer], fg,
                       final=(layer == depth - 1))
    return x
```

```python
import functools
import math

import jax
import jax.numpy as jnp
import numpy as np
from jax import lax
from jax.experimental import pallas as pl
from jax.experimental.pallas import tpu as pltpu

D_MODEL = 1024
N_BRANCH = 4
BR = D_MODEL // N_BRANCH
HEAD_DIM = 64
N_HEADS = BR // HEAD_DIM
CHUNK = 128
SHORT_CONV = 3
CONF_CONV = 31
EPS = 1e-6

LANES = 128
SUBLANES = 8
AUG = LANES
CONF_HALO = 32
SHORT_HALO = SUBLANES
CONV_ROWS = 64

T_MIX = 1024
T_ATT = 512
V_ROWS = HEAD_DIM + 16
LOG2E = math.log2(math.e)
SCORE_LOOKAHEAD = 2
T_OUT = 1024
PREP_COLS = 512
QKVF_COLS = 896
VMEM_LIMIT = 56 * 1024 * 1024

W_LO = 768
W_HI = 1540
H_BG, H_C, H_CG, H_D, H_DG, H_MERGE = 0, 256, 1024, 1280, 1792, 2048
H_BLOCK = 2048
F_HEAD, F_LANE = 1, 0
ONES_LANE = F_LANE + 3 * N_HEADS

NEG = -1e30

_NT = (((1,), (1,)), ((), ()))


def _dot(a, b):
    return jnp.dot(a, b, preferred_element_type=jnp.float32)


def _sigmoid(x):
    return 0.5 + 0.5 * jnp.tanh(0.5 * x)


def _silu(x):
    h = 0.5 * x
    return h + h * jnp.tanh(h)


def _gelu_tanh(x):
    c = math.sqrt(2.0 / math.pi)
    return 0.5 * x * (1.0 + jnp.tanh(c * (x + 0.044715 * (x * x * x))))


def _log_sigmoid(x):
    return jnp.minimum(x, 0.0) - jnp.log(1.0 + jnp.exp(-jnp.abs(x)))


def _layernorm(x, g, b):
    mu = jnp.mean(x, axis=-1, keepdims=True)
    xc = x - mu
    var = jnp.mean(xc * xc, axis=-1, keepdims=True)
    return xc * lax.rsqrt(var + EPS) * g + b


def _rms_scale(x):
    return x * lax.rsqrt(jnp.mean(x * x, axis=-1, keepdims=True) + EPS)


def _split3(x):
    hi = x.astype(jnp.bfloat16).astype(jnp.float32)
    r = x - hi
    mid = r.astype(jnp.bfloat16).astype(jnp.float32)
    lo = r - mid
    return hi, mid, lo


def _head_lanes(h):
    return (0, HEAD_DIM) if h % 2 == 0 else (HEAD_DIM, SUBLANES)


def _mixer_in_kernel(x_ref, g_ref, wa_ref, wqk_ref, wh_ref, wvt_ref, fb_ref, sguw_ref, sgub_ref,
                     aln_ref, scw_ref, dww_ref, dvec_ref, place_ref,
                     ya_ref, yc_ref, yd_ref, gb_ref, q_ref, k_ref, vt_ref,
                     zbuf, hbuf, cum_carry):
    t = x_ref.shape[1]
    j = pl.program_id(1)

    @pl.when(j == 0)
    def _():
        zbuf[0:SHORT_HALO, :] = jnp.zeros((SHORT_HALO, BR), jnp.float32)
        hbuf[0:CONF_HALO, :] = jnp.zeros((CONF_HALO, BR), jnp.float32)
        cum_carry[...] = jnp.zeros_like(cum_carry)

    x = x_ref[0]
    hb = (_rms_scale(x) * g_ref[...]).astype(jnp.bfloat16)


    glu = _dot(hb, wh_ref[0, :, H_D:H_DG])
    gd_lin = _dot(hb, wh_ref[0, :, H_DG:H_MERGE])
    hbuf[CONF_HALO:CONF_HALO + t, :] = glu[:, :BR] * _sigmoid(glu[:, BR:])

    def conf_rows(r0):
        dconv = jnp.broadcast_to(dvec_ref[0:1, :], (CONV_ROWS, BR))
        win = hbuf[r0:r0 + CONF_HALO + CONV_ROWS, :]
        for b in range(SUBLANES):
            wb = win if b == 0 else pltpu.roll(win, b, 0)
            for a in range(CONF_HALO // SUBLANES):
                s = SUBLANES * a + b
                if s < CONF_CONV:
                    kk = CONF_CONV - 1 - s
                    off = CONF_HALO - SUBLANES * a
                    wk = jnp.concatenate([dww_ref[kk]] * (CONV_ROWS // SUBLANES), axis=0)
                    dconv = dconv + wb[off:off + CONV_ROWS, :] * wk
        yd_ref[0, r0:r0 + CONV_ROWS, :] = (
            _silu(_layernorm(dconv, dvec_ref[1:2, :], dvec_ref[2:3, :]))
            * _silu(gd_lin[r0:r0 + CONV_ROWS])).astype(yd_ref.dtype)

    qa = _dot(hb, wqk_ref[0, :, 0:N_HEADS * AUG])
    ka = _dot(hb, wqk_ref[0, :, N_HEADS * AUG:])
    for r0 in range(0, t // 2, CONV_ROWS):
        conf_rows(r0)

    lf = _log_sigmoid(qa[:, F_HEAD * AUG:(F_HEAD + 1) * AUG] + fb_ref[...])
    r2 = lax.broadcasted_iota(jnp.int32, (CHUNK, CHUNK), 0)
    c2 = lax.broadcasted_iota(jnp.int32, (CHUNK, CHUNK), 1)
    tri = jnp.where(r2 >= c2, 1.0, 0.0).astype(jnp.bfloat16)
    lane_c = lax.broadcasted_iota(jnp.int32, (CHUNK, LANES), 1)
    grp = (lane_c - F_LANE) // N_HEADS
    within = []
    for c in range(t // CHUNK):
        hi, mid, lo = _split3(lf[c * CHUNK:(c + 1) * CHUNK])
        packed = jnp.where(grp == 0, hi,
                           jnp.where(grp == 1, pltpu.roll(mid, N_HEADS, 1),
                                     jnp.where(grp == 2, pltpu.roll(lo, 2 * N_HEADS, 1), 0.0)))
        r = _dot(tri, packed.astype(jnp.bfloat16))
        within.append(r + pltpu.roll(r, LANES - N_HEADS, 1)
                      + pltpu.roll(r, LANES - 2 * N_HEADS, 1))

    uv = _dot(hb, wa_ref[0, :, 0:2 * BR])
    ga_lin = _dot(hb, wa_ref[0, :, 2 * BR:W_LO])
    for r0 in range(t // 2, t, CONV_ROWS):
        conf_rows(r0)
    hbuf[0:CONF_HALO, :] = hbuf[t:t + CONF_HALO, :]

    cin = _dot(hb, wh_ref[0, :, H_C:H_CG])
    gc_lin = _dot(hb, wh_ref[0, :, H_CG:H_D])

    carry = cum_carry[...]
    cums = []
    for cc in within:
        cums.append(cc + carry)
        carry = carry + cc[CHUNK - 1:CHUNK, :]
    cum_carry[...] = carry
    cum = jnp.concatenate(cums, axis=0) * LOG2E

    hi, mid, lo = _split3(cum)
    lane_t = lax.broadcasted_iota(jnp.int32, (t, LANES), 1)
    grp_t = (lane_t - F_LANE) // N_HEADS
    packed = jnp.where(grp_t == 0, hi,
                       jnp.where(grp_t == 1, pltpu.roll(mid, N_HEADS, 1),
                                 jnp.where(grp_t == 2, pltpu.roll(lo, 2 * N_HEADS, 1),
                                           jnp.where(lane_t == ONES_LANE, 1.0, 0.0))))
    bias = _dot(packed.astype(jnp.bfloat16), place_ref[...])
    qa = qa + bias[:, :N_HEADS * AUG]
    ka = ka + bias[:, N_HEADS * AUG:]
    for h in range(N_HEADS):
        q_ref[0, h] = qa[:, h * AUG:(h + 1) * AUG].astype(q_ref.dtype)
        k_ref[0, h] = ka[:, h * AUG:(h + 1) * AUG].astype(k_ref.dtype)

    gb_lin = _dot(hb, wh_ref[0, :, H_BG:H_C])
    vt = lax.dot_general(wvt_ref[0], hb, _NT, preferred_element_type=jnp.float32)

    gel = _gelu_tanh(uv)
    u = gel[:, :BR]
    v = _layernorm(gel[:, BR:], aln_ref[0:1, :], aln_ref[1:2, :])
    rows = lax.broadcasted_iota(jnp.int32, (N_HEADS * CHUNK, CHUNK), 0)
    cols = lax.broadcasted_iota(jnp.int32, (N_HEADS * CHUNK, CHUNK), 1)
    wst = jnp.where((rows % CHUNK) >= cols, sguw_ref[...], 0.0).astype(jnp.bfloat16)
    malls = [_dot(wst, v[c * CHUNK:(c + 1) * CHUNK].astype(jnp.bfloat16))
             for c in range(t // CHUNK)]

    zbuf[SHORT_HALO:SHORT_HALO + t, :] = cin[:, BR:2 * BR] * cin[:, 2 * BR:]
    conv = jnp.zeros((t, BR), jnp.float32)
    for kk in range(SHORT_CONV):
        s = SHORT_CONV - 1 - kk
        conv = conv + zbuf[SHORT_HALO - s:SHORT_HALO - s + t, :] * scw_ref[kk:kk + 1, :]
    yc_ref[0] = (cin[:, :BR] * conv * _silu(gc_lin)).astype(yc_ref.dtype)
    zbuf[0:SHORT_HALO, :] = zbuf[t:t + SHORT_HALO, :]

    gb_ref[0] = _silu(gb_lin).astype(gb_ref.dtype)
    vt_ref[0] = vt.astype(vt_ref.dtype)

    lane_head = lax.broadcasted_iota(jnp.int32, (CHUNK, BR), 1) // HEAD_DIM
    ga = _silu(ga_lin)
    for c in range(t // CHUNK):
        sl = slice(c * CHUNK, (c + 1) * CHUNK)
        mixed = sgub_ref[...]
        for h in range(N_HEADS):
            mixed = mixed + jnp.where(lane_head == h, malls[c][h * CHUNK:(h + 1) * CHUNK], 0.0)
        ya_ref[0, sl, :] = (u[sl] * mixed * ga[sl]).astype(ya_ref.dtype)


def _mixer_in(layer, x, g, w_bf, wqk, w_hi, wvt, fb, sguw, sgub, aln, scw, dww, dvec, place):
    bsz, seq, d = x.shape
    t = T_MIX
    nblk = seq // t
    bf = jnp.bfloat16
    const = lambda shape: pl.BlockSpec(shape, lambda b, j: (0,) * len(shape))
    tok = lambda width: pl.BlockSpec((1, t, width), lambda b, j: (b, j, 0))
    return pl.pallas_call(
        _mixer_in_kernel,
        grid=(bsz, nblk),
        in_specs=[
            tok(d), const(g.shape),
            pl.BlockSpec((1, d, W_LO), lambda b, j: (layer, 0, 0)),
            pl.BlockSpec((1, d, 2 * N_HEADS * AUG), lambda b, j: (layer, 0, 0)),
            pl.BlockSpec((1, d, H_BLOCK), lambda b, j: (layer, 0, 0)),
            pl.BlockSpec((1, BR, d), lambda b, j: (layer, 0, 0)), const(fb.shape),
            const(sguw.shape), const(sgub.shape), const(aln.shape), const(scw.shape),
            const(dww.shape), const(dvec.shape), const(place.shape),
        ],
        out_specs=[
            tok(BR), tok(BR), tok(BR), tok(BR),
            pl.BlockSpec((1, N_HEADS, t, AUG), lambda b, j: (b, 0, j, 0)),
            pl.BlockSpec((1, N_HEADS, t, AUG), lambda b, j: (b, 0, j, 0)),
            pl.BlockSpec((1, BR, t), lambda b, j: (b, 0, j)),
        ],
        out_shape=[
            jax.ShapeDtypeStruct((bsz, seq, BR), bf),
            jax.ShapeDtypeStruct((bsz, seq, BR), bf),
            jax.ShapeDtypeStruct((bsz, seq, BR), bf),
            jax.ShapeDtypeStruct((bsz, seq, BR), bf),
            jax.ShapeDtypeStruct((bsz, N_HEADS, seq, AUG), bf),
            jax.ShapeDtypeStruct((bsz, N_HEADS, seq, AUG), bf),
            jax.ShapeDtypeStruct((bsz, BR, seq), bf),
        ],
        scratch_shapes=[
            pltpu.VMEM((SHORT_HALO + t, BR), jnp.float32),
            pltpu.VMEM((CONF_HALO + t, BR), jnp.float32),
            pltpu.VMEM((1, LANES), jnp.float32),
        ],
        compiler_params=pltpu.CompilerParams(
            dimension_semantics=("arbitrary", "arbitrary"), vmem_limit_bytes=VMEM_LIMIT),
        name="mixer_in",
    )(x, g, w_bf, wqk, w_hi, wvt, fb, sguw, sgub, aln, scw, dww, dvec, place)


def _fox_attn_kernel(q_ref, k_ref, vt_ref, gb_ref, o_ref, m_sc, acc_sc, s_sc):
    tq = q_ref.shape[2]
    tk = tq
    qi = pl.program_id(1)
    m_sc[...] = jnp.full(m_sc.shape, NEG, jnp.float32)
    acc_sc[...] = jnp.zeros(acc_sc.shape, jnp.float32)
    ones_rows = jnp.ones((V_ROWS - HEAD_DIM, tk), jnp.bfloat16)

    def scores(h, kt):
        ks = pl.multiple_of(kt * tk, tk)
        k = k_ref[0, h, pl.ds(ks, tk), :]
        return lax.dot_general(k, q_ref[0, h], _NT,
                               preferred_element_type=jnp.float32)

    def causal(s):
        kpos = lax.broadcasted_iota(jnp.int32, s.shape, 0)
        qpos = lax.broadcasted_iota(jnp.int32, s.shape, 1)
        return jnp.where(kpos <= qpos, s, NEG)

    def accumulate(h, kt, s):
        ks = pl.multiple_of(kt * tk, tk)
        m_old = m_sc[h]
        m_new = jnp.maximum(m_old, jnp.max(s, axis=0, keepdims=True))
        p = jnp.exp2(s - m_new).astype(jnp.bfloat16)
        vt = jnp.concatenate(
            [vt_ref[0, h * HEAD_DIM:(h + 1) * HEAD_DIM, pl.ds(ks, tk)], ones_rows], axis=0)
        acc_sc[h] = jnp.exp2(m_old - m_new) * acc_sc[h] + _dot(vt, p)
        m_sc[h] = m_new

    def step(kt):
        todo = [(h, kt) for h in range(1, N_HEADS)] + [(0, kt + 1)]
        ready = [s_sc[...]]
        for h in range(N_HEADS):
            while todo and len(ready) < 1 + SCORE_LOOKAHEAD:
                ready.append(scores(*todo.pop(0)))
            accumulate(h, kt, ready.pop(0))
        s_sc[...] = ready.pop(0)

    half = tk // 2

    def diag_scores(h, ks):
        if h == 0:
            s = s_sc[...]
            return s[:half], s[half:, half:]
        k_top = k_ref[0, h, pl.ds(ks, half), :]
        k_bot = k_ref[0, h, pl.ds(pl.multiple_of(ks + half, half), half), :]
        s_top = lax.dot_general(k_top, q_ref[0, h], _NT, preferred_element_type=jnp.float32)
        s_bot = lax.dot_general(k_bot, q_ref[0, h, half:, :], _NT,
                                preferred_element_type=jnp.float32)
        return s_top, s_bot

    def diag_accumulate(h, ks, s_top, s_bot):
        s_tl, s_tr, s_br = causal(s_top[:, :half]), s_top[:, half:], causal(s_bot)
        m_old = m_sc[h]
        m_l = jnp.maximum(m_old[:, :half], jnp.max(s_tl, axis=0, keepdims=True))
        m_r = jnp.maximum(
            m_old[:, half:],
            jnp.maximum(jnp.max(s_tr, axis=0, keepdims=True), jnp.max(s_br, axis=0, keepdims=True)))
        p_tl = jnp.exp2(s_tl - m_l).astype(jnp.bfloat16)
        p_tr = jnp.exp2(s_tr - m_r).astype(jnp.bfloat16)
        p_br = jnp.exp2(s_br - m_r).astype(jnp.bfloat16)
        rows = slice(h * HEAD_DIM, (h + 1) * HEAD_DIM)
        vt_top = jnp.concatenate([vt_ref[0, rows, pl.ds(ks, half)], ones_rows[:, :half]], axis=0)
        vt_bot = jnp.concatenate(
            [vt_ref[0, rows, pl.ds(pl.multiple_of(ks + half, half), half)], ones_rows[:, :half]],
            axis=0)
        acc = acc_sc[h]
        acc_l = jnp.exp2(m_old[:, :half] - m_l) * acc[:, :half] + _dot(vt_top, p_tl)
        acc_r = (jnp.exp2(m_old[:, half:] - m_r) * acc[:, half:]
                 + _dot(vt_top, p_tr) + _dot(vt_bot, p_br))
        acc_sc[h] = jnp.concatenate([acc_l, acc_r], axis=1)

    def diag_step(kt):
        ks = pl.multiple_of(kt * tk, tk)
        todo = list(range(1, N_HEADS))
        ready = [diag_scores(0, ks)]
        for h in range(N_HEADS):
            while todo and len(ready) < 1 + SCORE_LOOKAHEAD:
                ready.append(diag_scores(todo.pop(0), ks))
            diag_accumulate(h, ks, *ready.pop(0))

    def body(kt, carry):
        step(kt)
        return carry

    def body2(j, carry):
        step(2 * j)
        step(2 * j + 1)
        return carry

    s_sc[...] = scores(0, 0)
    pairs = qi // 2
    lax.fori_loop(0, pairs, body2, 0)
    lax.fori_loop(2 * pairs, qi, body, 0)
    diag_step(qi)
    outs = []
    for h in range(N_HEADS):
        acc = acc_sc[h]
        outs.append(acc[:HEAD_DIM] / acc[HEAD_DIM:HEAD_DIM + 1])
    o = jnp.concatenate(outs, axis=0).T
    o_ref[0] = (o * gb_ref[0].astype(jnp.float32)).astype(o_ref.dtype)


def _fox_attn(q_aug, k_aug, vt, gb):
    bsz, nh, seq, aug = q_aug.shape
    tq = T_ATT
    return pl.pallas_call(
        _fox_attn_kernel,
        grid=(bsz, seq // tq),
        in_specs=[
            pl.BlockSpec((1, nh, tq, aug), lambda b, i: (b, 0, i, 0)),
            pl.BlockSpec((1, nh, seq, aug), lambda b, i: (b, 0, 0, 0)),
            pl.BlockSpec((1, BR, seq), lambda b, i: (b, 0, 0)),
            pl.BlockSpec((1, tq, BR), lambda b, i: (b, i, 0)),
        ],
        out_specs=pl.BlockSpec((1, tq, BR), lambda b, i: (b, i, 0)),
        out_shape=jax.ShapeDtypeStruct((bsz, seq, BR), jnp.bfloat16),
        scratch_shapes=[
            pltpu.VMEM((nh, 1, tq), jnp.float32),
            pltpu.VMEM((nh, V_ROWS, tq), jnp.float32),
            pltpu.VMEM((tq, tq), jnp.float32),
        ],
        compiler_params=pltpu.CompilerParams(
            dimension_semantics=("arbitrary", "arbitrary"), vmem_limit_bytes=VMEM_LIMIT),
        name="fox_attn",
    )(q_aug, k_aug, vt, gb)


def _merge_out_kernel(x_ref, g_ref, ya_ref, yb_ref, yc_ref, yd_ref,
                      wm0_ref, wm1_ref, wb_ref, wo_ref, fg_ref, o_ref, *, final):
    x = x_ref[0]
    hb = (_rms_scale(x) * g_ref[...]).astype(jnp.bfloat16)
    ys = (ya_ref[0], yb_ref[0], yc_ref[0], yd_ref[0])
    per_ref = H_BLOCK // D_MODEL
    merged = jnp.zeros(x.shape, jnp.float32)
    for n in range(N_BRANCH):
        wm_ref = (wm0_ref, wm1_ref)[n // per_ref]
        c0 = (n % per_ref) * D_MODEL
        gate = _sigmoid(_dot(hb, wm_ref[0, :, c0:c0 + D_MODEL]))
        merged = merged + gate * _dot(ys[n], wb_ref[n])
    out = x + _dot(merged.astype(jnp.bfloat16), wo_ref[...])
    if final:
        out = _rms_scale(out) * fg_ref[...]
    o_ref[0] = out


def _merge_out(layer, x, g, ya, yb, yc, yd, w_hi, wb, wo, fg, final):
    bsz, seq, d = x.shape
    t = T_OUT
    const = lambda shape: pl.BlockSpec(shape, lambda b, j: (0,) * len(shape))
    tok = lambda width: pl.BlockSpec((1, t, width), lambda b, j: (b, j, 0))
    first = H_MERGE // H_BLOCK
    return pl.pallas_call(
        functools.partial(_merge_out_kernel, final=final),
        grid=(bsz, seq // t),
        in_specs=[
            tok(d), const(g.shape), tok(BR), tok(BR), tok(BR), tok(BR),
            pl.BlockSpec((1, d, H_BLOCK), lambda b, j: (layer, 0, first)),
            pl.BlockSpec((1, d, H_BLOCK), lambda b, j: (layer, 0, first + 1)),
            const(wb.shape), const(wo.shape), const(fg.shape),
        ],
        out_specs=tok(d),
        out_shape=jax.ShapeDtypeStruct((bsz, seq, d), jnp.float32),
        compiler_params=pltpu.CompilerParams(
            dimension_semantics=("arbitrary", "arbitrary"), vmem_limit_bytes=VMEM_LIMIT),
        name="merge_out_final" if final else "merge_out",
    )(x, g, ya, yb, yc, yd, w_hi, w_hi, wb, wo, fg)


def _weight_prep_kernel(wa_in, wq_in, wh_in, wa_ref, wqk_ref, wvt_ref, whi_ref):
    bf = jnp.bfloat16
    depth = wh_in.shape[1]
    for l in range(depth):
        whi_ref[l] = wh_in[:, l, :].T.astype(bf)

    @pl.when(pl.program_id(0) == 0)
    def _():
        lane = lax.broadcasted_iota(jnp.int32, (wa_in.shape[2], AUG), 1)
        for l in range(depth):
            wa_ref[l] = wa_in[:, l, :].T.astype(bf)
            xq = wq_in[:, l, :]
            wvt_ref[l] = xq[2 * BR:3 * BR].astype(bf)
            x = xq.T
            f_col = x[:, 3 * BR:3 * BR + LANES]
            for h in range(N_HEADS):
                lo, _ = _head_lanes(h)
                in_head = (lane >= lo) & (lane < lo + HEAD_DIM)
                c0 = (h // 2) * LANES
                q_col = x[:, c0:c0 + LANES] * (LOG2E / math.sqrt(HEAD_DIM))
                k_col = x[:, c0 + BR:c0 + BR + LANES]
                rest = jnp.where(lane < F_LANE + N_HEADS, f_col, 0.0) if h == F_HEAD else 0.0
                wqk_ref[l, :, h * AUG:(h + 1) * AUG] = jnp.where(in_head, q_col, rest).astype(bf)
                wqk_ref[l, :, (N_HEADS + h) * AUG:(N_HEADS + h + 1) * AUG] = (
                    jnp.where(in_head, k_col, 0.0).astype(bf))


def _weight_prep(w_in):
    depth, d, cols = w_in.shape
    bf = jnp.bfloat16
    w_t = jnp.transpose(w_in, (2, 0, 1))
    n_hi = (cols - W_HI) // PREP_COLS
    window = lambda n: (pl.Element(n), pl.Element(depth), pl.Element(d))
    return pl.pallas_call(
        _weight_prep_kernel,
        grid=(n_hi,),
        in_specs=[
            pl.BlockSpec(window(W_LO), lambda i: (0, 0, 0)),
            pl.BlockSpec(window(QKVF_COLS), lambda i: (W_LO, 0, 0)),
            pl.BlockSpec(window(PREP_COLS), lambda i: (W_HI + PREP_COLS * i, 0, 0)),
        ],
        out_specs=[pl.BlockSpec((depth, d, W_LO), lambda i: (0, 0, 0)),
                   pl.BlockSpec((depth, d, 2 * N_HEADS * AUG), lambda i: (0, 0, 0)),
                   pl.BlockSpec((depth, BR, d), lambda i: (0, 0, 0)),
                   pl.BlockSpec((depth, d, PREP_COLS), lambda i: (0, 0, i))],
        out_shape=[jax.ShapeDtypeStruct((depth, d, W_LO), bf),
                   jax.ShapeDtypeStruct((depth, d, 2 * N_HEADS * AUG), bf),
                   jax.ShapeDtypeStruct((depth, BR, d), bf),
                   jax.ShapeDtypeStruct((depth, d, cols - W_HI), bf)],
        compiler_params=pltpu.CompilerParams(
            dimension_semantics=("arbitrary",), vmem_limit_bytes=VMEM_LIMIT),
        name="weight_prep",
    )(w_t, w_t, w_t)


def _bias_placement():
    place = np.zeros((LANES, 2 * N_HEADS * AUG), np.float32)
    for h in range(N_HEADS):
        _, a0 = _head_lanes(h)
        qc, kc = h * AUG + a0, (N_HEADS + h) * AUG + a0
        for piece in range(3):
            src = F_LANE + piece * N_HEADS + h
            place[src, qc + piece] = 1.0
            place[src, kc + 3 + piece] = -1.0
            place[ONES_LANE, qc + 3 + piece] = 1.0
            place[ONES_LANE, kc + piece] = 1.0
    return jnp.asarray(place, jnp.bfloat16)


def kernel(x, norm_g, w_in, f_bias, sgu_w, sgu_b, sgu_ln_g, sgu_ln_b, short_conv_w,
           conf_dw_w, conf_dw_b, conf_ln_g, conf_ln_b, w_branch, w_out, final_g):
    depth = norm_g.shape[0]
    bf = jnp.bfloat16
    fg = final_g.reshape(1, D_MODEL)
    w_bf, wqk, wvt, w_hi = _weight_prep(w_in)
    wb_bf = w_branch.astype(bf)
    wo_bf = w_out.astype(bf)
    place = _bias_placement()
    for layer in range(depth):
        g = norm_g[layer].reshape(1, D_MODEL)
        fb = jnp.zeros((1, LANES), jnp.float32).at[0, F_LANE:F_LANE + N_HEADS].set(f_bias[layer])
        sguw = sgu_w[layer].reshape(N_HEADS * CHUNK, CHUNK)
        sgub = jnp.repeat(sgu_b[layer].T, HEAD_DIM, axis=1)
        aln = jnp.stack([sgu_ln_g[layer], sgu_ln_b[layer]])
        scw = jnp.zeros((SUBLANES, BR), jnp.float32).at[:SHORT_CONV].set(short_conv_w[layer])
        dww = jnp.broadcast_to(conf_dw_w[layer][:, None, :], (CONF_CONV, SUBLANES, BR))
        dvec = jnp.stack([conf_dw_b[layer], conf_ln_g[layer], conf_ln_b[layer]])
        ya, yc, yd, gb, q_aug, k_aug, vt = _mixer_in(
            layer, x, g, w_bf, wqk, w_hi, wvt, fb, sguw, sgub, aln, scw, dww, dvec, place)
        yb = _fox_attn(q_aug, k_aug, vt, gb)
        x = _merge_out(layer, x, g, ya, yb, yc, yd, w_hi, wb_bf[layer], wo_bf[layer], fg,
                       final=(layer == depth - 1))
    return x
```

```python
import functools
import math

import jax
import jax.numpy as jnp
import numpy as np
from jax import lax
from jax.experimental import pallas as pl
from jax.experimental.pallas import tpu as pltpu

D_MODEL = 1024
N_BRANCH = 4
BR = D_MODEL // N_BRANCH
HEAD_DIM = 64
N_HEADS = BR // HEAD_DIM
CHUNK = 128
SHORT_CONV = 3
CONF_CONV = 31
EPS = 1e-6

LANES = 128
SUBLANES = 8
AUG = LANES
CONF_HALO = 32
SHORT_HALO = SUBLANES
CONV_ROWS = 64

T_MIX = 1024
T_ATT = 512
V_ROWS = HEAD_DIM + 16
LOG2E = math.log2(math.e)
SCORE_LOOKAHEAD = 2
T_OUT = 1024
PREP_COLS = 512
QKVF_COLS = 896
VMEM_LIMIT = 56 * 1024 * 1024

W_LO = 768
W_HI = 1540
H_BG, H_C, H_CG, H_D, H_DG, H_MERGE = 0, 256, 1024, 1280, 1792, 2048
H_BLOCK = 2048
F_HEAD, F_LANE = 1, 0
ONES_LANE = F_LANE + 3 * N_HEADS

NEG = -1e30

_NT = (((1,), (1,)), ((), ()))


def _dot(a, b):
    return jnp.dot(a, b, preferred_element_type=jnp.float32)


def _sigmoid(x):
    return 0.5 + 0.5 * jnp.tanh(0.5 * x)


def _silu(x):
    h = 0.5 * x
    return h + h * jnp.tanh(h)


def _gelu_tanh(x):
    c = math.sqrt(2.0 / math.pi)
    return 0.5 * x * (1.0 + jnp.tanh(c * (x + 0.044715 * (x * x * x))))


def _log_sigmoid(x):
    return jnp.minimum(x, 0.0) - jnp.log(1.0 + jnp.exp(-jnp.abs(x)))


def _layernorm(x, g, b):
    mu = jnp.mean(x, axis=-1, keepdims=True)
    xc = x - mu
    var = jnp.mean(xc * xc, axis=-1, keepdims=True)
    return xc * lax.rsqrt(var + EPS) * g + b


def _rms_scale(x):
    return x * lax.rsqrt(jnp.mean(x * x, axis=-1, keepdims=True) + EPS)


def _split3(x):
    hi = x.astype(jnp.bfloat16).astype(jnp.float32)
    r = x - hi
    mid = r.astype(jnp.bfloat16).astype(jnp.float32)
    lo = r - mid
    return hi, mid, lo


def _head_lanes(h):
    return (0, HEAD_DIM) if h % 2 == 0 else (HEAD_DIM, SUBLANES)


def _mixer_in_kernel(x_ref, g_ref, wa_ref, wqk_ref, wh_ref, wvt_ref, fb_ref, sguw_ref, sgub_ref,
                     aln_ref, scw_ref, dww_ref, dvec_ref, place_ref,
                     ya_ref, yc_ref, yd_ref, gb_ref, q_ref, k_ref, vt_ref,
                     zbuf, hbuf, cum_carry):
    t = x_ref.shape[1]
    j = pl.program_id(1)

    @pl.when(j == 0)
    def _():
        zbuf[0:SHORT_HALO, :] = jnp.zeros((SHORT_HALO, BR), jnp.float32)
        hbuf[0:CONF_HALO, :] = jnp.zeros((CONF_HALO, BR), jnp.float32)
        cum_carry[...] = jnp.zeros_like(cum_carry)

    x = x_ref[0]
    hb = (_rms_scale(x) * g_ref[...]).astype(jnp.bfloat16)


    glu = _dot(hb, wh_ref[0, :, H_D:H_DG])
    gd_lin = _dot(hb, wh_ref[0, :, H_DG:H_MERGE])
    hbuf[CONF_HALO:CONF_HALO + t, :] = glu[:, :BR] * _sigmoid(glu[:, BR:])

    def conf_rows(r0):
        dconv = jnp.broadcast_to(dvec_ref[0:1, :], (CONV_ROWS, BR))
        win = hbuf[r0:r0 + CONF_HALO + CONV_ROWS, :]
        for b in range(SUBLANES):
            wb = win if b == 0 else pltpu.roll(win, b, 0)
            for a in range(CONF_HALO // SUBLANES):
                s = SUBLANES * a + b
                if s < CONF_CONV:
                    kk = CONF_CONV - 1 - s
                    off = CONF_HALO - SUBLANES * a
                    wk = jnp.concatenate([dww_ref[kk]] * (CONV_ROWS // SUBLANES), axis=0)
                    dconv = dconv + wb[off:off + CONV_ROWS, :] * wk
        yd_ref[0, r0:r0 + CONV_ROWS, :] = (
            _silu(_layernorm(dconv, dvec_ref[1:2, :], dvec_ref[2:3, :]))
            * _silu(gd_lin[r0:r0 + CONV_ROWS])).astype(yd_ref.dtype)

    qa = _dot(hb, wqk_ref[0, :, 0:N_HEADS * AUG])
    ka = _dot(hb, wqk_ref[0, :, N_HEADS * AUG:])
    for r0 in range(0, t // 2, CONV_ROWS):
        conf_rows(r0)

    lf = _log_sigmoid(qa[:, F_HEAD * AUG:(F_HEAD + 1) * AUG] + fb_ref[...])
    r2 = lax.broadcasted_iota(jnp.int32, (CHUNK, CHUNK), 0)
    c2 = lax.broadcasted_iota(jnp.int32, (CHUNK, CHUNK), 1)
    tri = jnp.where(r2 >= c2, 1.0, 0.0).astype(jnp.bfloat16)
    lane_c = lax.broadcasted_iota(jnp.int32, (CHUNK, LANES), 1)
    grp = (lane_c - F_LANE) // N_HEADS
    within = []
    for c in range(t // CHUNK):
        hi, mid, lo = _split3(lf[c * CHUNK:(c + 1) * CHUNK])
        packed = jnp.where(grp == 0, hi,
                           jnp.where(grp == 1, pltpu.roll(mid, N_HEADS, 1),
                                     jnp.where(grp == 2, pltpu.roll(lo, 2 * N_HEADS, 1), 0.0)))
        r = _dot(tri, packed.astype(jnp.bfloat16))
        within.append(r + pltpu.roll(r, LANES - N_HEADS, 1)
                      + pltpu.roll(r, LANES - 2 * N_HEADS, 1))

    uv = _dot(hb, wa_ref[0, :, 0:2 * BR])
    ga_lin = _dot(hb, wa_ref[0, :, 2 * BR:W_LO])
    for r0 in range(t // 2, t, CONV_ROWS):
        conf_rows(r0)
    hbuf[0:CONF_HALO, :] = hbuf[t:t + CONF_HALO, :]

    cin = _dot(hb, wh_ref[0, :, H_C:H_CG])
    gc_lin = _dot(hb, wh_ref[0, :, H_CG:H_D])

    carry = cum_carry[...]
    cums = []
    for cc in within:
        cums.append(cc + carry)
        carry = carry + cc[CHUNK - 1:CHUNK, :]
    cum_carry[...] = carry
    cum = jnp.concatenate(cums, axis=0) * LOG2E

    hi, mid, lo = _split3(cum)
    lane_t = lax.broadcasted_iota(jnp.int32, (t, LANES), 1)
    grp_t = (lane_t - F_LANE) // N_HEADS
    packed = jnp.where(grp_t == 0, hi,
                       jnp.where(grp_t == 1, pltpu.roll(mid, N_HEADS, 1),
                                 jnp.where(grp_t == 2, pltpu.roll(lo, 2 * N_HEADS, 1),
                                           jnp.where(lane_t == ONES_LANE, 1.0, 0.0))))
    bias = _dot(packed.astype(jnp.bfloat16), place_ref[...])
    qa = qa + bias[:, :N_HEADS * AUG]
    ka = ka + bias[:, N_HEADS * AUG:]
    for h in range(N_HEADS):
        q_ref[0, h] = qa[:, h * AUG:(h + 1) * AUG].astype(q_ref.dtype)
        k_ref[0, h] = ka[:, h * AUG:(h + 1) * AUG].astype(k_ref.dtype)

    gb_lin = _dot(hb, wh_ref[0, :, H_BG:H_C])
    vt = lax.dot_general(wvt_ref[0], hb, _NT, preferred_element_type=jnp.float32)

    gel = _gelu_tanh(uv)
    u = gel[:, :BR]
    v = _layernorm(gel[:, BR:], aln_ref[0:1, :], aln_ref[1:2, :])
    rows = lax.broadcasted_iota(jnp.int32, (N_HEADS * CHUNK, CHUNK), 0)
    cols = lax.broadcasted_iota(jnp.int32, (N_HEADS * CHUNK, CHUNK), 1)
    wst = jnp.where((rows % CHUNK) >= cols, sguw_ref[...], 0.0).astype(jnp.bfloat16)
    malls = [_dot(wst, v[c * CHUNK:(c + 1) * CHUNK].astype(jnp.bfloat16))
             for c in range(t // CHUNK)]

    zbuf[SHORT_HALO:SHORT_HALO + t, :] = cin[:, BR:2 * BR] * cin[:, 2 * BR:]
    conv = jnp.zeros((t, BR), jnp.float32)
    for kk in range(SHORT_CONV):
        s = SHORT_CONV - 1 - kk
        conv = conv + zbuf[SHORT_HALO - s:SHORT_HALO - s + t, :] * scw_ref[kk:kk + 1, :]
    yc_ref[0] = (cin[:, :BR] * conv * _silu(gc_lin)).astype(yc_ref.dtype)
    zbuf[0:SHORT_HALO, :] = zbuf[t:t + SHORT_HALO, :]

    gb_ref[0] = _silu(gb_lin).astype(gb_ref.dtype)
    vt_ref[0] = vt.astype(vt_ref.dtype)

    lane_head = lax.broadcasted_iota(jnp.int32, (CHUNK, BR), 1) // HEAD_DIM
    ga = _silu(ga_lin)
    for c in range(t // CHUNK):
        sl = slice(c * CHUNK, (c + 1) * CHUNK)
        mixed = sgub_ref[...]
        for h in range(N_HEADS):
            mixed = mixed + jnp.where(lane_head == h, malls[c][h * CHUNK:(h + 1) * CHUNK], 0.0)
        ya_ref[0, sl, :] = (u[sl] * mixed * ga[sl]).astype(ya_ref.dtype)


def _mixer_in(layer, x, g, w_bf, wqk, w_hi, wvt, fb, sguw, sgub, aln, scw, dww, dvec, place):
    bsz, seq, d = x.shape
    t = T_MIX
    nblk = seq // t
    bf = jnp.bfloat16
    const = lambda shape: pl.BlockSpec(shape, lambda b, j: (0,) * len(shape))
    tok = lambda width: pl.BlockSpec((1, t, width), lambda b, j: (b, j, 0))
    return pl.pallas_call(
        _mixer_in_kernel,
        grid=(bsz, nblk),
        in_specs=[
            tok(d), const(g.shape),
            pl.BlockSpec((1, d, W_LO), lambda b, j: (layer, 0, 0)),
            pl.BlockSpec((1, d, 2 * N_HEADS * AUG), lambda b, j: (layer, 0, 0)),
            pl.BlockSpec((1, d, H_BLOCK), lambda b, j: (layer, 0, 0)),
            pl.BlockSpec((1, BR, d), lambda b, j: (layer, 0, 0)), const(fb.shape),
            const(sguw.shape), const(sgub.shape), const(aln.shape), const(scw.shape),
            const(dww.shape), const(dvec.shape), const(place.shape),
        ],
        out_specs=[
            tok(BR), tok(BR), tok(BR), tok(BR),
            pl.BlockSpec((1, N_HEADS, t, AUG), lambda b, j: (b, 0, j, 0)),
            pl.BlockSpec((1, N_HEADS, t, AUG), lambda b, j: (b, 0, j, 0)),
            pl.BlockSpec((1, BR, t), lambda b, j: (b, 0, j)),
        ],
        out_shape=[
            jax.ShapeDtypeStruct((bsz, seq, BR), bf),
            jax.ShapeDtypeStruct((bsz, seq, BR), bf),
            jax.ShapeDtypeStruct((bsz, seq, BR), bf),
            jax.ShapeDtypeStruct((bsz, seq, BR), bf),
            jax.ShapeDtypeStruct((bsz, N_HEADS, seq, AUG), bf),
            jax.ShapeDtypeStruct((bsz, N_HEADS, seq, AUG), bf),
            jax.ShapeDtypeStruct((bsz, BR, seq), bf),
        ],
        scratch_shapes=[
            pltpu.VMEM((SHORT_HALO + t, BR), jnp.float32),
            pltpu.VMEM((CONF_HALO + t, BR), jnp.float32),
            pltpu.VMEM((1, LANES), jnp.float32),
        ],
        compiler_params=pltpu.CompilerParams(
            dimension_semantics=("arbitrary", "arbitrary"), vmem_limit_bytes=VMEM_LIMIT),
        name="mixer_in",
    )(x, g, w_bf, wqk, w_hi, wvt, fb, sguw, sgub, aln, scw, dww, dvec, place)


def _fox_attn_kernel(q_ref, k_ref, vt_ref, gb_ref, o_ref, m_sc, acc_sc, s_buf):
    tq = q_ref.shape[2]
    tk = tq
    qi = pl.program_id(1)
    m_sc[...] = jnp.full(m_sc.shape, NEG, jnp.float32)
    acc_sc[...] = jnp.zeros(acc_sc.shape, jnp.float32)
    ones_rows = jnp.ones((V_ROWS - HEAD_DIM, tk), jnp.bfloat16)

    def scores(h, kt, slot):
        ks = pl.multiple_of(kt * tk, tk)
        k = k_ref[0, h, pl.ds(ks, tk), :]
        s_buf[slot, :, 0:tq] = lax.dot_general(k, q_ref[0, h], _NT,
                                               preferred_element_type=jnp.float32)
        return slot

    def causal(s):
        kpos = lax.broadcasted_iota(jnp.int32, s.shape, 0)
        qpos = lax.broadcasted_iota(jnp.int32, s.shape, 1)
        return jnp.where(kpos <= qpos, s, NEG)

    def accumulate(h, kt, slot):
        s = s_buf[slot, :, 0:tq]
        ks = pl.multiple_of(kt * tk, tk)
        m_old = m_sc[h]
        m_new = jnp.maximum(m_old, jnp.max(s, axis=0, keepdims=True))
        p = jnp.exp2(s - m_new).astype(jnp.bfloat16)
        vt = jnp.concatenate(
            [vt_ref[0, h * HEAD_DIM:(h + 1) * HEAD_DIM, pl.ds(ks, tk)], ones_rows], axis=0)
        acc_sc[h] = jnp.exp2(m_old - m_new) * acc_sc[h] + _dot(vt, p)
        m_sc[h] = m_new

    def step(kt, parity):
        here, there = (0, N_HEADS) if parity == 0 else (N_HEADS, 0)
        todo = [(h, kt, h) for h in range(1, N_HEADS)] + [(0, kt + 1, there)]
        ready = [here]
        for h in range(N_HEADS):
            while todo and len(ready) < 1 + SCORE_LOOKAHEAD:
                ready.append(scores(*todo.pop(0)))
            accumulate(h, kt, ready.pop(0))

    half = tk // 2

    def diag_scores(h, ks):
        if h == 0:
            return N_HEADS * (qi % 2)
        k_top = k_ref[0, h, pl.ds(ks, half), :]
        k_bot = k_ref[0, h, pl.ds(pl.multiple_of(ks + half, half), half), :]
        s_buf[h, 0:half, 0:tq] = lax.dot_general(k_top, q_ref[0, h], _NT,
                                                 preferred_element_type=jnp.float32)
        s_buf[h, half:tk, half:tq] = lax.dot_general(k_bot, q_ref[0, h, half:, :], _NT,
                                                     preferred_element_type=jnp.float32)
        return h

    def diag_accumulate(h, ks, slot):
        s_tl = causal(s_buf[slot, 0:half, 0:half])
        s_tr = s_buf[slot, 0:half, half:tq]
        s_br = causal(s_buf[slot, half:tk, half:tq])
        m_old = m_sc[h]
        m_l = jnp.maximum(m_old[:, :half], jnp.max(s_tl, axis=0, keepdims=True))
        m_r = jnp.maximum(
            m_old[:, half:],
            jnp.maximum(jnp.max(s_tr, axis=0, keepdims=True), jnp.max(s_br, axis=0, keepdims=True)))
        p_tl = jnp.exp2(s_tl - m_l).astype(jnp.bfloat16)
        p_tr = jnp.exp2(s_tr - m_r).astype(jnp.bfloat16)
        p_br = jnp.exp2(s_br - m_r).astype(jnp.bfloat16)
        rows = slice(h * HEAD_DIM, (h + 1) * HEAD_DIM)
        vt_top = jnp.concatenate([vt_ref[0, rows, pl.ds(ks, half)], ones_rows[:, :half]], axis=0)
        vt_bot = jnp.concatenate(
            [vt_ref[0, rows, pl.ds(pl.multiple_of(ks + half, half), half)], ones_rows[:, :half]],
            axis=0)
        acc = acc_sc[h]
        acc_l = jnp.exp2(m_old[:, :half] - m_l) * acc[:, :half] + _dot(vt_top, p_tl)
        acc_r = (jnp.exp2(m_old[:, half:] - m_r) * acc[:, half:]
                 + _dot(vt_top, p_tr) + _dot(vt_bot, p_br))
        acc_sc[h] = jnp.concatenate([acc_l, acc_r], axis=1)

    def diag_step(kt):
        ks = pl.multiple_of(kt * tk, tk)
        todo = list(range(1, N_HEADS))
        ready = [diag_scores(0, ks)]
        for h in range(N_HEADS):
            while todo and len(ready) < 1 + SCORE_LOOKAHEAD:
                ready.append(diag_scores(todo.pop(0), ks))
            diag_accumulate(h, ks, ready.pop(0))

    def body(kt, carry):
        step(kt, 0)
        return carry

    def body2(j, carry):
        step(2 * j, 0)
        step(2 * j + 1, 1)
        return carry

    scores(0, 0, 0)
    pairs = qi // 2
    lax.fori_loop(0, pairs, body2, 0)
    lax.fori_loop(2 * pairs, qi, body, 0)
    diag_step(qi)
    outs = []
    for h in range(N_HEADS):
        acc = acc_sc[h]
        outs.append(acc[:HEAD_DIM] / acc[HEAD_DIM:HEAD_DIM + 1])
    o = jnp.concatenate(outs, axis=0).T
    o_ref[0] = (o * gb_ref[0].astype(jnp.float32)).astype(o_ref.dtype)


def _fox_attn(q_aug, k_aug, vt, gb):
    bsz, nh, seq, aug = q_aug.shape
    tq = T_ATT
    return pl.pallas_call(
        _fox_attn_kernel,
        grid=(bsz, seq // tq),
        in_specs=[
            pl.BlockSpec((1, nh, tq, aug), lambda b, i: (b, 0, i, 0)),
            pl.BlockSpec((1, nh, seq, aug), lambda b, i: (b, 0, 0, 0)),
            pl.BlockSpec((1, BR, seq), lambda b, i: (b, 0, 0)),
            pl.BlockSpec((1, tq, BR), lambda b, i: (b, i, 0)),
        ],
        out_specs=pl.BlockSpec((1, tq, BR), lambda b, i: (b, i, 0)),
        out_shape=jax.ShapeDtypeStruct((bsz, seq, BR), jnp.bfloat16),
        scratch_shapes=[
            pltpu.VMEM((nh, 1, tq), jnp.float32),
            pltpu.VMEM((nh, V_ROWS, tq), jnp.float32),
            pltpu.VMEM((nh + 1, tq, tq + LANES), jnp.float32),
        ],
        compiler_params=pltpu.CompilerParams(
            dimension_semantics=("arbitrary", "arbitrary"), vmem_limit_bytes=VMEM_LIMIT),
        name="fox_attn",
    )(q_aug, k_aug, vt, gb)


def _merge_out_kernel(x_ref, g_ref, ya_ref, yb_ref, yc_ref, yd_ref,
                      wm0_ref, wm1_ref, wb_ref, wo_ref, fg_ref, o_ref, *, final):
    x = x_ref[0]
    hb = (_rms_scale(x) * g_ref[...]).astype(jnp.bfloat16)
    ys = (ya_ref[0], yb_ref[0], yc_ref[0], yd_ref[0])
    per_ref = H_BLOCK // D_MODEL
    merged = jnp.zeros(x.shape, jnp.float32)
    for n in range(N_BRANCH):
        wm_ref = (wm0_ref, wm1_ref)[n // per_ref]
        c0 = (n % per_ref) * D_MODEL
        gate = _sigmoid(_dot(hb, wm_ref[0, :, c0:c0 + D_MODEL]))
        merged = merged + gate * _dot(ys[n], wb_ref[n])
    out = x + _dot(merged.astype(jnp.bfloat16), wo_ref[...])
    if final:
        out = _rms_scale(out) * fg_ref[...]
    o_ref[0] = out


def _merge_out(layer, x, g, ya, yb, yc, yd, w_hi, wb, wo, fg, final):
    bsz, seq, d = x.shape
    t = T_OUT
    const = lambda shape: pl.BlockSpec(shape, lambda b, j: (0,) * len(shape))
    tok = lambda width: pl.BlockSpec((1, t, width), lambda b, j: (b, j, 0))
    first = H_MERGE // H_BLOCK
    return pl.pallas_call(
        functools.partial(_merge_out_kernel, final=final),
        grid=(bsz, seq // t),
        in_specs=[
            tok(d), const(g.shape), tok(BR), tok(BR), tok(BR), tok(BR),
            pl.BlockSpec((1, d, H_BLOCK), lambda b, j: (layer, 0, first)),
            pl.BlockSpec((1, d, H_BLOCK), lambda b, j: (layer, 0, first + 1)),
            const(wb.shape), const(wo.shape), const(fg.shape),
        ],
        out_specs=tok(d),
        out_shape=jax.ShapeDtypeStruct((bsz, seq, d), jnp.float32),
        compiler_params=pltpu.CompilerParams(
            dimension_semantics=("arbitrary", "arbitrary"), vmem_limit_bytes=VMEM_LIMIT),
        name="merge_out_final" if final else "merge_out",
    )(x, g, ya, yb, yc, yd, w_hi, w_hi, wb, wo, fg)


def _weight_prep_kernel(wa_in, wq_in, wh_in, wa_ref, wqk_ref, wvt_ref, whi_ref):
    bf = jnp.bfloat16
    depth = wh_in.shape[1]
    for l in range(depth):
        whi_ref[l] = wh_in[:, l, :].T.astype(bf)

    @pl.when(pl.program_id(0) == 0)
    def _():
        lane = lax.broadcasted_iota(jnp.int32, (wa_in.shape[2], AUG), 1)
        for l in range(depth):
            wa_ref[l] = wa_in[:, l, :].T.astype(bf)
            xq = wq_in[:, l, :]
            wvt_ref[l] = xq[2 * BR:3 * BR].astype(bf)
            x = xq.T
            f_col = x[:, 3 * BR:3 * BR + LANES]
            for h in range(N_HEADS):
                lo, _ = _head_lanes(h)
                in_head = (lane >= lo) & (lane < lo + HEAD_DIM)
                c0 = (h // 2) * LANES
                q_col = x[:, c0:c0 + LANES] * (LOG2E / math.sqrt(HEAD_DIM))
                k_col = x[:, c0 + BR:c0 + BR + LANES]
                rest = jnp.where(lane < F_LANE + N_HEADS, f_col, 0.0) if h == F_HEAD else 0.0
                wqk_ref[l, :, h * AUG:(h + 1) * AUG] = jnp.where(in_head, q_col, rest).astype(bf)
                wqk_ref[l, :, (N_HEADS + h) * AUG:(N_HEADS + h + 1) * AUG] = (
                    jnp.where(in_head, k_col, 0.0).astype(bf))


def _weight_prep(w_in):
    depth, d, cols = w_in.shape
    bf = jnp.bfloat16
    w_t = jnp.transpose(w_in, (2, 0, 1))
    n_hi = (cols - W_HI) // PREP_COLS
    window = lambda n: (pl.Element(n), pl.Element(depth), pl.Element(d))
    return pl.pallas_call(
        _weight_prep_kernel,
        grid=(n_hi,),
        in_specs=[
            pl.BlockSpec(window(W_LO), lambda i: (0, 0, 0)),
            pl.BlockSpec(window(QKVF_COLS), lambda i: (W_LO, 0, 0)),
            pl.BlockSpec(window(PREP_COLS), lambda i: (W_HI + PREP_COLS * i, 0, 0)),
        ],
        out_specs=[pl.BlockSpec((depth, d, W_LO), lambda i: (0, 0, 0)),
                   pl.BlockSpec((depth, d, 2 * N_HEADS * AUG), lambda i: (0, 0, 0)),
                   pl.BlockSpec((depth, BR, d), lambda i: (0, 0, 0)),
                   pl.BlockSpec((depth, d, PREP_COLS), lambda i: (0, 0, i))],
        out_shape=[jax.ShapeDtypeStruct((depth, d, W_LO), bf),
                   jax.ShapeDtypeStruct((depth, d, 2 * N_HEADS * AUG), bf),
                   jax.ShapeDtypeStruct((depth, BR, d), bf),
                   jax.ShapeDtypeStruct((depth, d, cols - W_HI), bf)],
        compiler_params=pltpu.CompilerParams(
            dimension_semantics=("arbitrary",), vmem_limit_bytes=VMEM_LIMIT),
        name="weight_prep",
    )(w_t, w_t, w_t)


def _bias_placement():
    place = np.zeros((LANES, 2 * N_HEADS * AUG), np.float32)
    for h in range(N_HEADS):
        _, a0 = _head_lanes(h)
        qc, kc = h * AUG + a0, (N_HEADS + h) * AUG + a0
        for piece in range(3):
            src = F_LANE + piece * N_HEADS + h
            place[src, qc + piece] = 1.0
            place[src, kc + 3 + piece] = -1.0
            place[ONES_LANE, qc + 3 + piece] = 1.0
            place[ONES_LANE, kc + piece] = 1.0
    return jnp.asarray(place, jnp.bfloat16)


def kernel(x, norm_g, w_in, f_bias, sgu_w, sgu_b, sgu_ln_g, sgu_ln_b, short_conv_w,
           conf_dw_w, conf_dw_b, conf_ln_g, conf_ln_b, w_branch, w_out, final_g):
    depth = norm_g.shape[0]
    bsz, seq, d_model = x.shape
    assert d_model == D_MODEL and w_in.shape[2] == W_HI + H_MERGE + N_BRANCH * D_MODEL
    assert seq % T_MIX == 0 and seq % T_ATT == 0 and seq % T_OUT == 0
    bf = jnp.bfloat16
    fg = final_g.reshape(1, D_MODEL)
    w_bf, wqk, wvt, w_hi = _weight_prep(w_in)
    wb_bf = w_branch.astype(bf)
    wo_bf = w_out.astype(bf)
    place = _bias_placement()
    for layer in range(depth):
        g = norm_g[layer].reshape(1, D_MODEL)
        fb = jnp.zeros((1, LANES), jnp.float32).at[0, F_LANE:F_LANE + N_HEADS].set(f_bias[layer])
        sguw = sgu_w[layer].reshape(N_HEADS * CHUNK, CHUNK)
        sgub = jnp.repeat(sgu_b[layer].T, HEAD_DIM, axis=1)
        aln = jnp.stack([sgu_ln_g[layer], sgu_ln_b[layer]])
        scw = jnp.zeros((SUBLANES, BR), jnp.float32).at[:SHORT_CONV].set(short_conv_w[layer])
        dww = jnp.broadcast_to(conf_dw_w[layer][:, None, :], (CONF_CONV, SUBLANES, BR))
        dvec = jnp.stack([conf_dw_b[layer], conf_ln_g[layer], conf_ln_b[layer]])
        ya, yc, yd, gb, q_aug, k_aug, vt = _mixer_in(
            layer, x, g, w_bf, wqk, w_hi, wvt, fb, sguw, sgub, aln, scw, dww, dvec, place)
        yb = _fox_attn(q_aug, k_aug, vt, gb)
        x = _merge_out(layer, x, g, ya, yb, yc, yd, w_hi, wb_bf[layer], wo_bf[layer], fg,
                       final=(layer == depth - 1))
    return x
```

```python
import functools
import math

import jax
import jax.numpy as jnp
import numpy as np
from jax import lax
from jax.experimental import pallas as pl
from jax.experimental.pallas import tpu as pltpu

D_MODEL = 1024
N_BRANCH = 4
BR = D_MODEL // N_BRANCH
HEAD_DIM = 64
N_HEADS = BR // HEAD_DIM
CHUNK = 128
SHORT_CONV = 3
CONF_CONV = 31
EPS = 1e-6

LANES = 128
SUBLANES = 8
AUG = LANES
CONF_HALO = 32
SHORT_HALO = SUBLANES
CONV_ROWS = 32

T_MIX = 1024
T_ATT = 512
V_ROWS = HEAD_DIM + 16
LOG2E = math.log2(math.e)
SCORE_LOOKAHEAD = 2
T_OUT = 1024
PREP_COLS = 512
QKVF_COLS = 896
VMEM_LIMIT = 56 * 1024 * 1024

W_LO = 768
W_HI = 1540
H_BG, H_C, H_CG, H_D, H_DG, H_MERGE = 0, 256, 1024, 1280, 1792, 2048
H_BLOCK = 2048
F_HEAD, F_LANE = 1, 0
ONES_LANE = F_LANE + 3 * N_HEADS

NEG = -1e30

_NT = (((1,), (1,)), ((), ()))


def _dot(a, b):
    return jnp.dot(a, b, preferred_element_type=jnp.float32)


def _sigmoid(x):
    return 0.5 + 0.5 * jnp.tanh(0.5 * x)


def _silu(x):
    h = 0.5 * x
    return h + h * jnp.tanh(h)


def _gelu_tanh(x):
    c = math.sqrt(2.0 / math.pi)
    return 0.5 * x * (1.0 + jnp.tanh(c * (x + 0.044715 * (x * x * x))))


def _log_sigmoid(x):
    return jnp.minimum(x, 0.0) - jnp.log(1.0 + jnp.exp(-jnp.abs(x)))


def _layernorm(x, g, b):
    mu = jnp.mean(x, axis=-1, keepdims=True)
    xc = x - mu
    var = jnp.mean(xc * xc, axis=-1, keepdims=True)
    return xc * lax.rsqrt(var + EPS) * g + b


def _rms_scale(x):
    return x * lax.rsqrt(jnp.mean(x * x, axis=-1, keepdims=True) + EPS)


def _split3(x):
    hi = x.astype(jnp.bfloat16).astype(jnp.float32)
    r = x - hi
    mid = r.astype(jnp.bfloat16).astype(jnp.float32)
    lo = r - mid
    return hi, mid, lo


def _head_lanes(h):
    return (0, HEAD_DIM) if h % 2 == 0 else (HEAD_DIM, SUBLANES)


def _mixer_in_kernel(x_ref, g_ref, wa_ref, wqk_ref, wh_ref, wvt_ref, fb_ref, sguw_ref, sgub_ref,
                     aln_ref, scw_ref, dww_ref, dvec_ref, place_ref,
                     ya_ref, yc_ref, yd_ref, gb_ref, q_ref, k_ref, vt_ref,
                     zbuf, hbuf, cum_carry):
    t = x_ref.shape[1]
    j = pl.program_id(1)

    @pl.when(j == 0)
    def _():
        zbuf[0:SHORT_HALO, :] = jnp.zeros((SHORT_HALO, BR), jnp.float32)
        hbuf[0:CONF_HALO, :] = jnp.zeros((CONF_HALO, BR), jnp.float32)
        cum_carry[...] = jnp.zeros_like(cum_carry)

    x = x_ref[0]
    hb = (_rms_scale(x) * g_ref[...]).astype(jnp.bfloat16)


    glu = _dot(hb, wh_ref[0, :, H_D:H_DG])
    gd_lin = _dot(hb, wh_ref[0, :, H_DG:H_MERGE])
    hbuf[CONF_HALO:CONF_HALO + t, :] = glu[:, :BR] * _sigmoid(glu[:, BR:])

    def conf_rows(r0):
        dconv = jnp.broadcast_to(dvec_ref[0:1, :], (CONV_ROWS, BR))
        win = hbuf[r0:r0 + CONF_HALO + CONV_ROWS, :]
        for b in range(SUBLANES):
            wb = win if b == 0 else pltpu.roll(win, b, 0)
            for a in range(CONF_HALO // SUBLANES):
                s = SUBLANES * a + b
                if s < CONF_CONV:
                    kk = CONF_CONV - 1 - s
                    off = CONF_HALO - SUBLANES * a
                    wk = jnp.concatenate([dww_ref[kk]] * (CONV_ROWS // SUBLANES), axis=0)
                    dconv = dconv + wb[off:off + CONV_ROWS, :] * wk
        yd_ref[0, r0:r0 + CONV_ROWS, :] = (
            _silu(_layernorm(dconv, dvec_ref[1:2, :], dvec_ref[2:3, :]))
            * _silu(gd_lin[r0:r0 + CONV_ROWS])).astype(yd_ref.dtype)

    qa = _dot(hb, wqk_ref[0, :, 0:N_HEADS * AUG])
    ka = _dot(hb, wqk_ref[0, :, N_HEADS * AUG:])
    for r0 in range(0, t // 4, CONV_ROWS):
        conf_rows(r0)
    uv = _dot(hb, wa_ref[0, :, 0:2 * BR])
    ga_lin = _dot(hb, wa_ref[0, :, 2 * BR:W_LO])
    for r0 in range(t // 4, t // 2, CONV_ROWS):
        conf_rows(r0)

    lf = _log_sigmoid(qa[:, F_HEAD * AUG:(F_HEAD + 1) * AUG] + fb_ref[...])
    r2 = lax.broadcasted_iota(jnp.int32, (CHUNK, CHUNK), 0)
    c2 = lax.broadcasted_iota(jnp.int32, (CHUNK, CHUNK), 1)
    tri = jnp.where(r2 >= c2, 1.0, 0.0).astype(jnp.bfloat16)
    lane_c = lax.broadcasted_iota(jnp.int32, (CHUNK, LANES), 1)
    grp = (lane_c - F_LANE) // N_HEADS
    within = []
    for c in range(t // CHUNK):
        hi, mid, lo = _split3(lf[c * CHUNK:(c + 1) * CHUNK])
        packed = jnp.where(grp == 0, hi,
                           jnp.where(grp == 1, pltpu.roll(mid, N_HEADS, 1),
                                     jnp.where(grp == 2, pltpu.roll(lo, 2 * N_HEADS, 1), 0.0)))
        r = _dot(tri, packed.astype(jnp.bfloat16))
        within.append(r + pltpu.roll(r, LANES - N_HEADS, 1)
                      + pltpu.roll(r, LANES - 2 * N_HEADS, 1))

    cin = _dot(hb, wh_ref[0, :, H_C:H_CG])
    gc_lin = _dot(hb, wh_ref[0, :, H_CG:H_D])
    for r0 in range(t // 2, 3 * t // 4, CONV_ROWS):
        conf_rows(r0)
    gb_lin = _dot(hb, wh_ref[0, :, H_BG:H_C])
    vt = lax.dot_general(wvt_ref[0], hb, _NT, preferred_element_type=jnp.float32)
    for r0 in range(3 * t // 4, t, CONV_ROWS):
        conf_rows(r0)
    hbuf[0:CONF_HALO, :] = hbuf[t:t + CONF_HALO, :]

    carry = cum_carry[...]
    cums = []
    for cc in within:
        cums.append(cc + carry)
        carry = carry + cc[CHUNK - 1:CHUNK, :]
    cum_carry[...] = carry
    cum = jnp.concatenate(cums, axis=0) * LOG2E

    hi, mid, lo = _split3(cum)
    lane_t = lax.broadcasted_iota(jnp.int32, (t, LANES), 1)
    grp_t = (lane_t - F_LANE) // N_HEADS
    packed = jnp.where(grp_t == 0, hi,
                       jnp.where(grp_t == 1, pltpu.roll(mid, N_HEADS, 1),
                                 jnp.where(grp_t == 2, pltpu.roll(lo, 2 * N_HEADS, 1),
                                           jnp.where(lane_t == ONES_LANE, 1.0, 0.0))))
    bias = _dot(packed.astype(jnp.bfloat16), place_ref[...])
    qa = qa + bias[:, :N_HEADS * AUG]
    ka = ka + bias[:, N_HEADS * AUG:]
    for h in range(N_HEADS):
        q_ref[0, h] = qa[:, h * AUG:(h + 1) * AUG].astype(q_ref.dtype)
        k_ref[0, h] = ka[:, h * AUG:(h + 1) * AUG].astype(k_ref.dtype)

    gel = _gelu_tanh(uv)
    u = gel[:, :BR]
    v = _layernorm(gel[:, BR:], aln_ref[0:1, :], aln_ref[1:2, :])
    rows = lax.broadcasted_iota(jnp.int32, (N_HEADS * CHUNK, CHUNK), 0)
    cols = lax.broadcasted_iota(jnp.int32, (N_HEADS * CHUNK, CHUNK), 1)
    wst = jnp.where((rows % CHUNK) >= cols, sguw_ref[...], 0.0).astype(jnp.bfloat16)
    malls = [_dot(wst, v[c * CHUNK:(c + 1) * CHUNK].astype(jnp.bfloat16))
             for c in range(t // CHUNK)]

    zbuf[SHORT_HALO:SHORT_HALO + t, :] = cin[:, BR:2 * BR] * cin[:, 2 * BR:]
    conv = jnp.zeros((t, BR), jnp.float32)
    for kk in range(SHORT_CONV):
        s = SHORT_CONV - 1 - kk
        conv = conv + zbuf[SHORT_HALO - s:SHORT_HALO - s + t, :] * scw_ref[kk:kk + 1, :]
    yc_ref[0] = (cin[:, :BR] * conv * _silu(gc_lin)).astype(yc_ref.dtype)
    zbuf[0:SHORT_HALO, :] = zbuf[t:t + SHORT_HALO, :]

    gb_ref[0] = _silu(gb_lin).astype(gb_ref.dtype)
    vt_ref[0] = vt.astype(vt_ref.dtype)

    lane_head = lax.broadcasted_iota(jnp.int32, (CHUNK, BR), 1) // HEAD_DIM
    ga = _silu(ga_lin)
    for c in range(t // CHUNK):
        sl = slice(c * CHUNK, (c + 1) * CHUNK)
        mixed = sgub_ref[...]
        for h in range(N_HEADS):
            mixed = mixed + jnp.where(lane_head == h, malls[c][h * CHUNK:(h + 1) * CHUNK], 0.0)
        ya_ref[0, sl, :] = (u[sl] * mixed * ga[sl]).astype(ya_ref.dtype)


def _mixer_in(layer, x, g, w_bf, wqk, w_hi, wvt, fb, sguw, sgub, aln, scw, dww, dvec, place):
    bsz, seq, d = x.shape
    t = T_MIX
    nblk = seq // t
    bf = jnp.bfloat16
    const = lambda shape: pl.BlockSpec(shape, lambda b, j: (0,) * len(shape))
    tok = lambda width: pl.BlockSpec((1, t, width), lambda b, j: (b, j, 0))
    return pl.pallas_call(
        _mixer_in_kernel,
        grid=(bsz, nblk),
        in_specs=[
            tok(d), const(g.shape),
            pl.BlockSpec((1, d, W_LO), lambda b, j: (layer, 0, 0)),
            pl.BlockSpec((1, d, 2 * N_HEADS * AUG), lambda b, j: (layer, 0, 0)),
            pl.BlockSpec((1, d, H_BLOCK), lambda b, j: (layer, 0, 0)),
            pl.BlockSpec((1, BR, d), lambda b, j: (layer, 0, 0)), const(fb.shape),
            const(sguw.shape), const(sgub.shape), const(aln.shape), const(scw.shape),
            const(dww.shape), const(dvec.shape), const(place.shape),
        ],
        out_specs=[
            tok(BR), tok(BR), tok(BR), tok(BR),
            pl.BlockSpec((1, N_HEADS, t, AUG), lambda b, j: (b, 0, j, 0)),
            pl.BlockSpec((1, N_HEADS, t, AUG), lambda b, j: (b, 0, j, 0)),
            pl.BlockSpec((1, BR, t), lambda b, j: (b, 0, j)),
        ],
        out_shape=[
            jax.ShapeDtypeStruct((bsz, seq, BR), bf),
            jax.ShapeDtypeStruct((bsz, seq, BR), bf),
            jax.ShapeDtypeStruct((bsz, seq, BR), bf),
            jax.ShapeDtypeStruct((bsz, seq, BR), bf),
            jax.ShapeDtypeStruct((bsz, N_HEADS, seq, AUG), bf),
            jax.ShapeDtypeStruct((bsz, N_HEADS, seq, AUG), bf),
            jax.ShapeDtypeStruct((bsz, BR, seq), bf),
        ],
        scratch_shapes=[
            pltpu.VMEM((SHORT_HALO + t, BR), jnp.float32),
            pltpu.VMEM((CONF_HALO + t, BR), jnp.float32),
            pltpu.VMEM((1, LANES), jnp.float32),
        ],
        compiler_params=pltpu.CompilerParams(
            dimension_semantics=("arbitrary", "arbitrary"), vmem_limit_bytes=VMEM_LIMIT),
        name="mixer_in",
    )(x, g, w_bf, wqk, w_hi, wvt, fb, sguw, sgub, aln, scw, dww, dvec, place)


def _fox_attn_kernel(q_ref, k_ref, vt_ref, gb_ref, o_ref, m_sc, acc_sc, s_buf):
    tq = q_ref.shape[2]
    tk = tq
    qi = pl.program_id(1)
    m_sc[...] = jnp.full(m_sc.shape, NEG, jnp.float32)
    acc_sc[...] = jnp.zeros(acc_sc.shape, jnp.float32)
    ones_rows = jnp.ones((V_ROWS - HEAD_DIM, tk), jnp.bfloat16)

    def scores(h, kt, slot):
        ks = pl.multiple_of(kt * tk, tk)
        k = k_ref[0, h, pl.ds(ks, tk), :]
        s_buf[slot, :, 0:tq] = lax.dot_general(k, q_ref[0, h], _NT,
                                               preferred_element_type=jnp.float32)
        return slot

    def causal(s):
        kpos = lax.broadcasted_iota(jnp.int32, s.shape, 0)
        qpos = lax.broadcasted_iota(jnp.int32, s.shape, 1)
        return jnp.where(kpos <= qpos, s, NEG)

    def accumulate(h, kt, slot):
        s = s_buf[slot, :, 0:tq]
        ks = pl.multiple_of(kt * tk, tk)
        m_old = m_sc[h]
        m_new = jnp.maximum(m_old, jnp.max(s, axis=0, keepdims=True))
        p = jnp.exp2(s - m_new).astype(jnp.bfloat16)
        vt = jnp.concatenate(
            [vt_ref[0, h * HEAD_DIM:(h + 1) * HEAD_DIM, pl.ds(ks, tk)], ones_rows], axis=0)
        acc_sc[h] = jnp.exp2(m_old - m_new) * acc_sc[h] + _dot(vt, p)
        m_sc[h] = m_new

    def step(kt, parity):
        here, there = (0, N_HEADS) if parity == 0 else (N_HEADS, 0)
        todo = [(h, kt, h) for h in range(1, N_HEADS)] + [(0, kt + 1, there)]
        ready = [here]
        for h in range(N_HEADS):
            while todo and len(ready) < 1 + SCORE_LOOKAHEAD:
                ready.append(scores(*todo.pop(0)))
            accumulate(h, kt, ready.pop(0))

    half = tk // 2

    def diag_scores(h, ks):
        if h == 0:
            return N_HEADS * (qi % 2)
        k_top = k_ref[0, h, pl.ds(ks, half), :]
        k_bot = k_ref[0, h, pl.ds(pl.multiple_of(ks + half, half), half), :]
        s_buf[h, 0:half, 0:tq] = lax.dot_general(k_top, q_ref[0, h], _NT,
                                                 preferred_element_type=jnp.float32)
        s_buf[h, half:tk, half:tq] = lax.dot_general(k_bot, q_ref[0, h, half:, :], _NT,
                                                     preferred_element_type=jnp.float32)
        return h

    def diag_accumulate(h, ks, slot):
        s_tl = causal(s_buf[slot, 0:half, 0:half])
        s_tr = s_buf[slot, 0:half, half:tq]
        s_br = causal(s_buf[slot, half:tk, half:tq])
        m_old = m_sc[h]
        m_l = jnp.maximum(m_old[:, :half], jnp.max(s_tl, axis=0, keepdims=True))
        m_r = jnp.maximum(
            m_old[:, half:],
            jnp.maximum(jnp.max(s_tr, axis=0, keepdims=True), jnp.max(s_br, axis=0, keepdims=True)))
        p_tl = jnp.exp2(s_tl - m_l).astype(jnp.bfloat16)
        p_tr = jnp.exp2(s_tr - m_r).astype(jnp.bfloat16)
        p_br = jnp.exp2(s_br - m_r).astype(jnp.bfloat16)
        rows = slice(h * HEAD_DIM, (h + 1) * HEAD_DIM)
        vt_top = jnp.concatenate([vt_ref[0, rows, pl.ds(ks, half)], ones_rows[:, :half]], axis=0)
        vt_bot = jnp.concatenate(
            [vt_ref[0, rows, pl.ds(pl.multiple_of(ks + half, half), half)], ones_rows[:, :half]],
            axis=0)
        acc = acc_sc[h]
        acc_l = jnp.exp2(m_old[:, :half] - m_l) * acc[:, :half] + _dot(vt_top, p_tl)
        acc_r = (jnp.exp2(m_old[:, half:] - m_r) * acc[:, half:]
                 + _dot(vt_top, p_tr) + _dot(vt_bot, p_br))
        acc_sc[h] = jnp.concatenate([acc_l, acc_r], axis=1)

    def diag_step(kt):
        ks = pl.multiple_of(kt * tk, tk)
        todo = list(range(1, N_HEADS))
        ready = [diag_scores(0, ks)]
        for h in range(N_HEADS):
            while todo and len(ready) < 1 + SCORE_LOOKAHEAD:
                ready.append(diag_scores(todo.pop(0), ks))
            diag_accumulate(h, ks, ready.pop(0))

    def body(kt, carry):
        step(kt, 0)
        return carry

    def body2(j, carry):
        step(2 * j, 0)
        step(2 * j + 1, 1)
        return carry

    scores(0, 0, 0)
    pairs = qi // 2
    lax.fori_loop(0, pairs, body2, 0)
    lax.fori_loop(2 * pairs, qi, body, 0)
    diag_step(qi)
    outs = []
    for h in range(N_HEADS):
        acc = acc_sc[h]
        outs.append(acc[:HEAD_DIM] / acc[HEAD_DIM:HEAD_DIM + 1])
    o = jnp.concatenate(outs, axis=0).T
    o_ref[0] = (o * gb_ref[0].astype(jnp.float32)).astype(o_ref.dtype)


def _fox_attn(q_aug, k_aug, vt, gb):
    bsz, nh, seq, aug = q_aug.shape
    tq = T_ATT
    return pl.pallas_call(
        _fox_attn_kernel,
        grid=(bsz, seq // tq),
        in_specs=[
            pl.BlockSpec((1, nh, tq, aug), lambda b, i: (b, 0, i, 0)),
            pl.BlockSpec((1, nh, seq, aug), lambda b, i: (b, 0, 0, 0)),
            pl.BlockSpec((1, BR, seq), lambda b, i: (b, 0, 0)),
            pl.BlockSpec((1, tq, BR), lambda b, i: (b, i, 0)),
        ],
        out_specs=pl.BlockSpec((1, tq, BR), lambda b, i: (b, i, 0)),
        out_shape=jax.ShapeDtypeStruct((bsz, seq, BR), jnp.bfloat16),
        scratch_shapes=[
            pltpu.VMEM((nh, 1, tq), jnp.float32),
            pltpu.VMEM((nh, V_ROWS, tq), jnp.float32),
            pltpu.VMEM((nh + 1, tq, tq + LANES), jnp.float32),
        ],
        compiler_params=pltpu.CompilerParams(
            dimension_semantics=("arbitrary", "arbitrary"), vmem_limit_bytes=VMEM_LIMIT),
        name="fox_attn",
    )(q_aug, k_aug, vt, gb)


def _merge_out_kernel(x_ref, g_ref, ya_ref, yb_ref, yc_ref, yd_ref,
                      wm0_ref, wm1_ref, wb_ref, wo_ref, fg_ref, o_ref, *, final):
    x = x_ref[0]
    hb = (_rms_scale(x) * g_ref[...]).astype(jnp.bfloat16)
    ys = (ya_ref[0], yb_ref[0], yc_ref[0], yd_ref[0])
    per_ref = H_BLOCK // D_MODEL
    merged = jnp.zeros(x.shape, jnp.float32)
    for n in range(N_BRANCH):
        wm_ref = (wm0_ref, wm1_ref)[n // per_ref]
        c0 = (n % per_ref) * D_MODEL
        gate = _sigmoid(_dot(hb, wm_ref[0, :, c0:c0 + D_MODEL]))
        merged = merged + gate * _dot(ys[n], wb_ref[n])
    out = x + _dot(merged.astype(jnp.bfloat16), wo_ref[...])
    if final:
        out = _rms_scale(out) * fg_ref[...]
    o_ref[0] = out


def _merge_out(layer, x, g, ya, yb, yc, yd, w_hi, wb, wo, fg, final):
    bsz, seq, d = x.shape
    t = T_OUT
    const = lambda shape: pl.BlockSpec(shape, lambda b, j: (0,) * len(shape))
    tok = lambda width: pl.BlockSpec((1, t, width), lambda b, j: (b, j, 0))
    first = H_MERGE // H_BLOCK
    return pl.pallas_call(
        functools.partial(_merge_out_kernel, final=final),
        grid=(bsz, seq // t),
        in_specs=[
            tok(d), const(g.shape), tok(BR), tok(BR), tok(BR), tok(BR),
            pl.BlockSpec((1, d, H_BLOCK), lambda b, j: (layer, 0, first)),
            pl.BlockSpec((1, d, H_BLOCK), lambda b, j: (layer, 0, first + 1)),
            const(wb.shape), const(wo.shape), const(fg.shape),
        ],
        out_specs=tok(d),
        out_shape=jax.ShapeDtypeStruct((bsz, seq, d), jnp.float32),
        compiler_params=pltpu.CompilerParams(
            dimension_semantics=("arbitrary", "arbitrary"), vmem_limit_bytes=VMEM_LIMIT),
        name="merge_out_final" if final else "merge_out",
    )(x, g, ya, yb, yc, yd, w_hi, w_hi, wb, wo, fg)


def _weight_prep_kernel(wa_in, wq_in, wh_in, wa_ref, wqk_ref, wvt_ref, whi_ref):
    bf = jnp.bfloat16
    depth = wh_in.shape[1]
    for l in range(depth):
        whi_ref[l] = wh_in[:, l, :].T.astype(bf)

    @pl.when(pl.program_id(0) == 0)
    def _():
        lane = lax.broadcasted_iota(jnp.int32, (wa_in.shape[2], AUG), 1)
        for l in range(depth):
            wa_ref[l] = wa_in[:, l, :].T.astype(bf)
            xq = wq_in[:, l, :]
            wvt_ref[l] = xq[2 * BR:3 * BR].astype(bf)
            x = xq.T
            f_col = x[:, 3 * BR:3 * BR + LANES]
            for h in range(N_HEADS):
                lo, _ = _head_lanes(h)
                in_head = (lane >= lo) & (lane < lo + HEAD_DIM)
                c0 = (h // 2) * LANES
                q_col = x[:, c0:c0 + LANES] * (LOG2E / math.sqrt(HEAD_DIM))
                k_col = x[:, c0 + BR:c0 + BR + LANES]
                rest = jnp.where(lane < F_LANE + N_HEADS, f_col, 0.0) if h == F_HEAD else 0.0
                wqk_ref[l, :, h * AUG:(h + 1) * AUG] = jnp.where(in_head, q_col, rest).astype(bf)
                wqk_ref[l, :, (N_HEADS + h) * AUG:(N_HEADS + h + 1) * AUG] = (
                    jnp.where(in_head, k_col, 0.0).astype(bf))


def _weight_prep(w_in):
    depth, d, cols = w_in.shape
    bf = jnp.bfloat16
    w_t = jnp.transpose(w_in, (2, 0, 1))
    n_hi = (cols - W_HI) // PREP_COLS
    window = lambda n: (pl.Element(n), pl.Element(depth), pl.Element(d))
    return pl.pallas_call(
        _weight_prep_kernel,
        grid=(n_hi,),
        in_specs=[
            pl.BlockSpec(window(W_LO), lambda i: (0, 0, 0)),
            pl.BlockSpec(window(QKVF_COLS), lambda i: (W_LO, 0, 0)),
            pl.BlockSpec(window(PREP_COLS), lambda i: (W_HI + PREP_COLS * i, 0, 0)),
        ],
        out_specs=[pl.BlockSpec((depth, d, W_LO), lambda i: (0, 0, 0)),
                   pl.BlockSpec((depth, d, 2 * N_HEADS * AUG), lambda i: (0, 0, 0)),
                   pl.BlockSpec((depth, BR, d), lambda i: (0, 0, 0)),
                   pl.BlockSpec((depth, d, PREP_COLS), lambda i: (0, 0, i))],
        out_shape=[jax.ShapeDtypeStruct((depth, d, W_LO), bf),
                   jax.ShapeDtypeStruct((depth, d, 2 * N_HEADS * AUG), bf),
                   jax.ShapeDtypeStruct((depth, BR, d), bf),
                   jax.ShapeDtypeStruct((depth, d, cols - W_HI), bf)],
        compiler_params=pltpu.CompilerParams(
            dimension_semantics=("arbitrary",), vmem_limit_bytes=VMEM_LIMIT),
        name="weight_prep",
    )(w_t, w_t, w_t)


def _bias_placement():
    place = np.zeros((LANES, 2 * N_HEADS * AUG), np.float32)
    for h in range(N_HEADS):
        _, a0 = _head_lanes(h)
        qc, kc = h * AUG + a0, (N_HEADS + h) * AUG + a0
        for piece in range(3):
            src = F_LANE + piece * N_HEADS + h
            place[src, qc + piece] = 1.0
            place[src, kc + 3 + piece] = -1.0
            place[ONES_LANE, qc + 3 + piece] = 1.0
            place[ONES_LANE, kc + piece] = 1.0
    return jnp.asarray(place, jnp.bfloat16)


def kernel(x, norm_g, w_in, f_bias, sgu_w, sgu_b, sgu_ln_g, sgu_ln_b, short_conv_w,
           conf_dw_w, conf_dw_b, conf_ln_g, conf_ln_b, w_branch, w_out, final_g):
    depth = norm_g.shape[0]
    bsz, seq, d_model = x.shape
    assert d_model == D_MODEL and w_in.shape[2] == W_HI + H_MERGE + N_BRANCH * D_MODEL
    assert seq % T_MIX == 0 and seq % T_ATT == 0 and seq % T_OUT == 0
    bf = jnp.bfloat16
    fg = final_g.reshape(1, D_MODEL)
    w_bf, wqk, wvt, w_hi = _weight_prep(w_in)
    wb_bf = w_branch.astype(bf)
    wo_bf = w_out.astype(bf)
    place = _bias_placement()
    for layer in range(depth):
        g = norm_g[layer].reshape(1, D_MODEL)
        fb = jnp.zeros((1, LANES), jnp.float32).at[0, F_LANE:F_LANE + N_HEADS].set(f_bias[layer])
        sguw = sgu_w[layer].reshape(N_HEADS * CHUNK, CHUNK)
        sgub = jnp.repeat(sgu_b[layer].T, HEAD_DIM, axis=1)
        aln = jnp.stack([sgu_ln_g[layer], sgu_ln_b[layer]])
        scw = jnp.zeros((SUBLANES, BR), jnp.float32).at[:SHORT_CONV].set(short_conv_w[layer])
        dww = jnp.broadcast_to(conf_dw_w[layer][:, None, :], (CONF_CONV, SUBLANES, BR))
        dvec = jnp.stack([conf_dw_b[layer], conf_ln_g[layer], conf_ln_b[layer]])
        ya, yc, yd, gb, q_aug, k_aug, vt = _mixer_in(
            layer, x, g, w_bf, wqk, w_hi, wvt, fb, sguw, sgub, aln, scw, dww, dvec, place)
        yb = _fox_attn(q_aug, k_aug, vt, gb)
        x = _merge_out(layer, x, g, ya, yb, yc, yd, w_hi, wb_bf[layer], wo_bf[layer], fg,
                       final=(layer == depth - 1))
    return x
```

```python
import functools
import math

import jax
import jax.numpy as jnp
import numpy as np
from jax import lax
from jax.experimental import pallas as pl
from jax.experimental.pallas import tpu as pltpu

D_MODEL = 1024
N_BRANCH = 4
BR = D_MODEL // N_BRANCH
HEAD_DIM = 64
N_HEADS = BR // HEAD_DIM
CHUNK = 128
SHORT_CONV = 3
CONF_CONV = 31
EPS = 1e-6

LANES = 128
SUBLANES = 8
AUG = LANES
CONF_HALO = 32
SHORT_HALO = SUBLANES
CONV_ROWS = 64

T_MIX = 1024
T_ATT = 512
V_ROWS = HEAD_DIM + 16
LOG2E = math.log2(math.e)
SCORE_LOOKAHEAD = 2
T_OUT = 1024
PREP_COLS = 512
QKVF_COLS = 896
VMEM_LIMIT = 56 * 1024 * 1024

W_LO = 768
W_HI = 1540
H_BG, H_C, H_CG, H_D, H_DG, H_MERGE = 0, 256, 1024, 1280, 1792, 2048
H_BLOCK = 2048
F_HEAD, F_LANE = 1, 0
ONES_LANE = F_LANE + 3 * N_HEADS

NEG = -1e30

_NT = (((1,), (1,)), ((), ()))


def _dot(a, b):
    return jnp.dot(a, b, preferred_element_type=jnp.float32)


def _sigmoid(x):
    return 0.5 + 0.5 * jnp.tanh(0.5 * x)


def _silu(x):
    h = 0.5 * x
    return h + h * jnp.tanh(h)


def _gelu_tanh(x):
    c = math.sqrt(2.0 / math.pi)
    return 0.5 * x * (1.0 + jnp.tanh(c * (x + 0.044715 * (x * x * x))))


def _log_sigmoid(x):
    return jnp.minimum(x, 0.0) - jnp.log(1.0 + jnp.exp(-jnp.abs(x)))


def _layernorm(x, g, b):
    mu = jnp.mean(x, axis=-1, keepdims=True)
    xc = x - mu
    var = jnp.mean(xc * xc, axis=-1, keepdims=True)
    return xc * lax.rsqrt(var + EPS) * g + b


def _rms_scale(x):
    return x * lax.rsqrt(jnp.mean(x * x, axis=-1, keepdims=True) + EPS)


def _split3(x):
    hi = x.astype(jnp.bfloat16).astype(jnp.float32)
    r = x - hi
    mid = r.astype(jnp.bfloat16).astype(jnp.float32)
    lo = r - mid
    return hi, mid, lo


def _head_lanes(h):
    return (0, HEAD_DIM) if h % 2 == 0 else (HEAD_DIM, SUBLANES)


def _mixer_in_kernel(x_ref, g_ref, wa_ref, wqk_ref, wh_ref, wvt_ref, fb_ref, sguw_ref, sgub_ref,
                     aln_ref, scw_ref, dww_ref, dvec_ref, place_ref,
                     ya_ref, yc_ref, yd_ref, gb_ref, q_ref, k_ref, vt_ref,
                     zbuf, hbuf, cum_carry):
    t = x_ref.shape[1]
    j = pl.program_id(1)

    @pl.when(j == 0)
    def _():
        zbuf[0:SHORT_HALO, :] = jnp.zeros((SHORT_HALO, BR), jnp.float32)
        hbuf[0:CONF_HALO, :] = jnp.zeros((CONF_HALO, BR), jnp.float32)
        cum_carry[...] = jnp.zeros_like(cum_carry)

    x = x_ref[0]
    hb = (_rms_scale(x) * g_ref[...]).astype(jnp.bfloat16)


    glu = _dot(hb, wh_ref[0, :, H_D:H_DG])
    gd_lin = _dot(hb, wh_ref[0, :, H_DG:H_MERGE])
    hbuf[CONF_HALO:CONF_HALO + t, :] = glu[:, :BR] * _sigmoid(glu[:, BR:])

    def conf_rows(r0):
        dconv = jnp.broadcast_to(dvec_ref[0:1, :], (CONV_ROWS, BR))
        win = hbuf[r0:r0 + CONF_HALO + CONV_ROWS, :]
        for b in range(SUBLANES):
            wb = win if b == 0 else pltpu.roll(win, b, 0)
            for a in range(CONF_HALO // SUBLANES):
                s = SUBLANES * a + b
                if s < CONF_CONV:
                    kk = CONF_CONV - 1 - s
                    off = CONF_HALO - SUBLANES * a
                    wk = jnp.concatenate([dww_ref[kk]] * (CONV_ROWS // SUBLANES), axis=0)
                    dconv = dconv + wb[off:off + CONV_ROWS, :] * wk
        yd_ref[0, r0:r0 + CONV_ROWS, :] = (
            _silu(_layernorm(dconv, dvec_ref[1:2, :], dvec_ref[2:3, :]))
            * _silu(gd_lin[r0:r0 + CONV_ROWS])).astype(yd_ref.dtype)

    qa = _dot(hb, wqk_ref[0, :, 0:N_HEADS * AUG])
    ka = _dot(hb, wqk_ref[0, :, N_HEADS * AUG:])
    for r0 in range(0, t // 2, CONV_ROWS):
        conf_rows(r0)

    lf = _log_sigmoid(qa[:, F_HEAD * AUG:(F_HEAD + 1) * AUG] + fb_ref[...])
    r2 = lax.broadcasted_iota(jnp.int32, (CHUNK, CHUNK), 0)
    c2 = lax.broadcasted_iota(jnp.int32, (CHUNK, CHUNK), 1)
    tri = jnp.where(r2 >= c2, 1.0, 0.0).astype(jnp.bfloat16)
    lane_c = lax.broadcasted_iota(jnp.int32, (CHUNK, LANES), 1)
    grp = (lane_c - F_LANE) // N_HEADS
    within = []
    for c in range(t // CHUNK):
        hi, mid, lo = _split3(lf[c * CHUNK:(c + 1) * CHUNK])
        packed = jnp.where(grp == 0, hi,
                           jnp.where(grp == 1, pltpu.roll(mid, N_HEADS, 1),
                                     jnp.where(grp == 2, pltpu.roll(lo, 2 * N_HEADS, 1), 0.0)))
        r = _dot(tri, packed.astype(jnp.bfloat16))
        within.append(r + pltpu.roll(r, LANES - N_HEADS, 1)
                      + pltpu.roll(r, LANES - 2 * N_HEADS, 1))

    uv = _dot(hb, wa_ref[0, :, 0:2 * BR])
    ga_lin = _dot(hb, wa_ref[0, :, 2 * BR:W_LO])
    for r0 in range(t // 2, t, CONV_ROWS):
        conf_rows(r0)
    hbuf[0:CONF_HALO, :] = hbuf[t:t + CONF_HALO, :]

    cin = _dot(hb, wh_ref[0, :, H_C:H_CG])
    gc_lin = _dot(hb, wh_ref[0, :, H_CG:H_D])

    carry = cum_carry[...]
    cums = []
    for cc in within:
        cums.append(cc + carry)
        carry = carry + cc[CHUNK - 1:CHUNK, :]
    cum_carry[...] = carry
    cum = jnp.concatenate(cums, axis=0) * LOG2E

    hi, mid, lo = _split3(cum)
    lane_t = lax.broadcasted_iota(jnp.int32, (t, LANES), 1)
    grp_t = (lane_t - F_LANE) // N_HEADS
    packed = jnp.where(grp_t == 0, hi,
                       jnp.where(grp_t == 1, pltpu.roll(mid, N_HEADS, 1),
                                 jnp.where(grp_t == 2, pltpu.roll(lo, 2 * N_HEADS, 1),
                                           jnp.where(lane_t == ONES_LANE, 1.0, 0.0))))
    bias = _dot(packed.astype(jnp.bfloat16), place_ref[...])
    qa = qa + bias[:, :N_HEADS * AUG]
    ka = ka + bias[:, N_HEADS * AUG:]
    for h in range(N_HEADS):
        q_ref[0, h] = qa[:, h * AUG:(h + 1) * AUG].astype(q_ref.dtype)
        k_ref[0, h] = ka[:, h * AUG:(h + 1) * AUG].astype(k_ref.dtype)

    gb_lin = _dot(hb, wh_ref[0, :, H_BG:H_C])
    vt = lax.dot_general(wvt_ref[0], hb, _NT, preferred_element_type=jnp.float32)

    gel = _gelu_tanh(uv)
    u = gel[:, :BR]
    v = _layernorm(gel[:, BR:], aln_ref[0:1, :], aln_ref[1:2, :])
    rows = lax.broadcasted_iota(jnp.int32, (N_HEADS * CHUNK, CHUNK), 0)
    cols = lax.broadcasted_iota(jnp.int32, (N_HEADS * CHUNK, CHUNK), 1)
    wst = jnp.where((rows % CHUNK) >= cols, sguw_ref[...], 0.0).astype(jnp.bfloat16)
    malls = [_dot(wst, v[c * CHUNK:(c + 1) * CHUNK].astype(jnp.bfloat16))
             for c in range(t // CHUNK)]

    zbuf[SHORT_HALO:SHORT_HALO + t, :] = cin[:, BR:2 * BR] * cin[:, 2 * BR:]
    conv = jnp.zeros((t, BR), jnp.float32)
    for kk in range(SHORT_CONV):
        s = SHORT_CONV - 1 - kk
        conv = conv + zbuf[SHORT_HALO - s:SHORT_HALO - s + t, :] * scw_ref[kk:kk + 1, :]
    yc_ref[0] = (cin[:, :BR] * conv * _silu(gc_lin)).astype(yc_ref.dtype)
    zbuf[0:SHORT_HALO, :] = zbuf[t:t + SHORT_HALO, :]

    gb_ref[0] = _silu(gb_lin).astype(gb_ref.dtype)
    vt_ref[0] = vt.astype(vt_ref.dtype)

    lane_head = lax.broadcasted_iota(jnp.int32, (CHUNK, BR), 1) // HEAD_DIM
    ga = _silu(ga_lin)
    for c in range(t // CHUNK):
        sl = slice(c * CHUNK, (c + 1) * CHUNK)
        mixed = sgub_ref[...]
        for h in range(N_HEADS):
            mixed = mixed + jnp.where(lane_head == h, malls[c][h * CHUNK:(h + 1) * CHUNK], 0.0)
        ya_ref[0, sl, :] = (u[sl] * mixed * ga[sl]).astype(ya_ref.dtype)


def _mixer_in(layer, x, g, w_bf, wqk, w_hi, wvt, fb, sguw, sgub, aln, scw, dww, dvec, place):
    bsz, seq, d = x.shape
    t = T_MIX
    nblk = seq // t
    bf = jnp.bfloat16
    const = lambda shape: pl.BlockSpec(shape, lambda b, j: (0,) * len(shape))
    tok = lambda width: pl.BlockSpec((1, t, width), lambda b, j: (b, j, 0))
    return pl.pallas_call(
        _mixer_in_kernel,
        grid=(bsz, nblk),
        in_specs=[
            tok(d), const(g.shape),
            pl.BlockSpec((1, d, W_LO), lambda b, j: (layer, 0, 0)),
            pl.BlockSpec((1, d, 2 * N_HEADS * AUG), lambda b, j: (layer, 0, 0)),
            pl.BlockSpec((1, d, H_BLOCK), lambda b, j: (layer, 0, 0)),
            pl.BlockSpec((1, BR, d), lambda b, j: (layer, 0, 0)), const(fb.shape),
            const(sguw.shape), const(sgub.shape), const(aln.shape), const(scw.shape),
            const(dww.shape), const(dvec.shape), const(place.shape),
        ],
        out_specs=[
            tok(BR), tok(BR), tok(BR), tok(BR),
            pl.BlockSpec((1, N_HEADS, t, AUG), lambda b, j: (b, 0, j, 0)),
            pl.BlockSpec((1, N_HEADS, t, AUG), lambda b, j: (b, 0, j, 0)),
            pl.BlockSpec((1, BR, t), lambda b, j: (b, 0, j)),
        ],
        out_shape=[
            jax.ShapeDtypeStruct((bsz, seq, BR), bf),
            jax.ShapeDtypeStruct((bsz, seq, BR), bf),
            jax.ShapeDtypeStruct((bsz, seq, BR), bf),
            jax.ShapeDtypeStruct((bsz, seq, BR), bf),
            jax.ShapeDtypeStruct((bsz, N_HEADS, seq, AUG), bf),
            jax.ShapeDtypeStruct((bsz, N_HEADS, seq, AUG), bf),
            jax.ShapeDtypeStruct((bsz, BR, seq), bf),
        ],
        scratch_shapes=[
            pltpu.VMEM((SHORT_HALO + t, BR), jnp.float32),
            pltpu.VMEM((CONF_HALO + t, BR), jnp.float32),
            pltpu.VMEM((1, LANES), jnp.float32),
        ],
        compiler_params=pltpu.CompilerParams(
            dimension_semantics=("arbitrary", "arbitrary"), vmem_limit_bytes=VMEM_LIMIT),
        name="mixer_in",
    )(x, g, w_bf, wqk, w_hi, wvt, fb, sguw, sgub, aln, scw, dww, dvec, place)


def _fox_attn_kernel(q_ref, k_ref, vt_ref, gb_ref, o_ref, m_sc, acc_sc, s_buf):
    tq = q_ref.shape[2]
    tk = tq
    qi = pl.program_id(1)
    m_sc[...] = jnp.full(m_sc.shape, NEG, jnp.float32)
    acc_sc[...] = jnp.zeros(acc_sc.shape, jnp.float32)
    ones_rows = jnp.ones((V_ROWS - HEAD_DIM, tk), jnp.bfloat16)

    def scores(h, kt, slot):
        ks = pl.multiple_of(kt * tk, tk)
        k = k_ref[0, h, pl.ds(ks, tk), :]
        s_buf[slot, :, 0:tq] = lax.dot_general(k, q_ref[0, h], _NT,
                                               preferred_element_type=jnp.float32)
        return slot

    def causal(s):
        kpos = lax.broadcasted_iota(jnp.int32, s.shape, 0)
        qpos = lax.broadcasted_iota(jnp.int32, s.shape, 1)
        return jnp.where(kpos <= qpos, s, NEG)

    def accumulate(h, kt, slot):
        s = s_buf[slot, :, 0:tq]
        ks = pl.multiple_of(kt * tk, tk)
        m_old = m_sc[h]
        m_new = jnp.maximum(m_old, jnp.max(s, axis=0, keepdims=True))
        p = jnp.exp2(s - m_new).astype(jnp.bfloat16)
        vt = jnp.concatenate(
            [vt_ref[0, h * HEAD_DIM:(h + 1) * HEAD_DIM, pl.ds(ks, tk)], ones_rows], axis=0)
        acc_sc[h] = jnp.exp2(m_old - m_new) * acc_sc[h] + _dot(vt, p)
        m_sc[h] = m_new

    def step(kt, parity):
        here, there = (0, N_HEADS) if parity == 0 else (N_HEADS, 0)
        todo = [(h, kt, h) for h in range(1, N_HEADS)] + [(0, kt + 1, there)]
        ready = [here]
        for h in range(N_HEADS):
            while todo and len(ready) < 1 + SCORE_LOOKAHEAD:
                ready.append(scores(*todo.pop(0)))
            accumulate(h, kt, ready.pop(0))

    half = tk // 2

    def diag_scores(h, ks):
        if h == 0:
            return N_HEADS * (qi % 2)
        k_top = k_ref[0, h, pl.ds(ks, half), :]
        k_bot = k_ref[0, h, pl.ds(pl.multiple_of(ks + half, half), half), :]
        s_buf[h, 0:half, 0:tq] = lax.dot_general(k_top, q_ref[0, h], _NT,
                                                 preferred_element_type=jnp.float32)
        s_buf[h, half:tk, half:tq] = lax.dot_general(k_bot, q_ref[0, h, half:, :], _NT,
                                                     preferred_element_type=jnp.float32)
        return h

    def diag_accumulate(h, ks, slot):
        s_tl = causal(s_buf[slot, 0:half, 0:half])
        s_tr = s_buf[slot, 0:half, half:tq]
        s_br = causal(s_buf[slot, half:tk, half:tq])
        m_old = m_sc[h]
        m_l = jnp.maximum(m_old[:, :half], jnp.max(s_tl, axis=0, keepdims=True))
        m_r = jnp.maximum(
            m_old[:, half:],
            jnp.maximum(jnp.max(s_tr, axis=0, keepdims=True), jnp.max(s_br, axis=0, keepdims=True)))
        p_tl = jnp.exp2(s_tl - m_l).astype(jnp.bfloat16)
        p_tr = jnp.exp2(s_tr - m_r).astype(jnp.bfloat16)
        p_br = jnp.exp2(s_br - m_r).astype(jnp.bfloat16)
        rows = slice(h * HEAD_DIM, (h + 1) * HEAD_DIM)
        vt_top = jnp.concatenate([vt_ref[0, rows, pl.ds(ks, half)], ones_rows[:, :half]], axis=0)
        vt_bot = jnp.concatenate(
            [vt_ref[0, rows, pl.ds(pl.multiple_of(ks + half, half), half)], ones_rows[:, :half]],
            axis=0)
        acc = acc_sc[h]
        acc_l = jnp.exp2(m_old[:, :half] - m_l) * acc[:, :half] + _dot(vt_top, p_tl)
        acc_r = (jnp.exp2(m_old[:, half:] - m_r) * acc[:, half:]
                 + _dot(vt_top, p_tr) + _dot(vt_bot, p_br))
        acc_sc[h] = jnp.concatenate([acc_l, acc_r], axis=1)

    def diag_step(kt):
        ks = pl.multiple_of(kt * tk, tk)
        todo = list(range(1, N_HEADS))
        ready = [diag_scores(0, ks)]
        for h in range(N_HEADS):
            while todo and len(ready) < 1 + SCORE_LOOKAHEAD:
                ready.append(diag_scores(todo.pop(0), ks))
            diag_accumulate(h, ks, ready.pop(0))

    def body(kt, carry):
        step(kt, 0)
        return carry

    def body2(j, carry):
        step(2 * j, 0)
        step(2 * j + 1, 1)
        return carry

    scores(0, 0, 0)
    pairs = qi // 2
    lax.fori_loop(0, pairs, body2, 0)
    lax.fori_loop(2 * pairs, qi, body, 0)
    diag_step(qi)
    outs = []
    for h in range(N_HEADS):
        acc = acc_sc[h]
        outs.append(acc[:HEAD_DIM] / acc[HEAD_DIM:HEAD_DIM + 1])
    o = jnp.concatenate(outs, axis=0).T
    o_ref[0] = (o * gb_ref[0].astype(jnp.float32)).astype(o_ref.dtype)


def _fox_attn(q_aug, k_aug, vt, gb):
    bsz, nh, seq, aug = q_aug.shape
    tq = T_ATT
    return pl.pallas_call(
        _fox_attn_kernel,
        grid=(bsz, seq // tq),
        in_specs=[
            pl.BlockSpec((1, nh, tq, aug), lambda b, i: (b, 0, i, 0)),
            pl.BlockSpec((1, nh, seq, aug), lambda b, i: (b, 0, 0, 0)),
            pl.BlockSpec((1, BR, seq), lambda b, i: (b, 0, 0)),
            pl.BlockSpec((1, tq, BR), lambda b, i: (b, i, 0)),
        ],
        out_specs=pl.BlockSpec((1, tq, BR), lambda b, i: (b, i, 0)),
        out_shape=jax.ShapeDtypeStruct((bsz, seq, BR), jnp.bfloat16),
        scratch_shapes=[
            pltpu.VMEM((nh, 1, tq), jnp.float32),
            pltpu.VMEM((nh, V_ROWS, tq), jnp.float32),
            pltpu.VMEM((nh + 1, tq, tq), jnp.float32),
        ],
        compiler_params=pltpu.CompilerParams(
            dimension_semantics=("arbitrary", "arbitrary"), vmem_limit_bytes=VMEM_LIMIT),
        name="fox_attn",
    )(q_aug, k_aug, vt, gb)


def _merge_out_kernel(x_ref, g_ref, ya_ref, yb_ref, yc_ref, yd_ref,
                      wm0_ref, wm1_ref, wb_ref, wo_ref, fg_ref, o_ref, *, final):
    x = x_ref[0]
    hb = (_rms_scale(x) * g_ref[...]).astype(jnp.bfloat16)
    ys = (ya_ref[0], yb_ref[0], yc_ref[0], yd_ref[0])
    per_ref = H_BLOCK // D_MODEL
    merged = jnp.zeros(x.shape, jnp.float32)
    for n in range(N_BRANCH):
        wm_ref = (wm0_ref, wm1_ref)[n // per_ref]
        c0 = (n % per_ref) * D_MODEL
        gate = _sigmoid(_dot(hb, wm_ref[0, :, c0:c0 + D_MODEL]))
        merged = merged + gate * _dot(ys[n], wb_ref[n])
    out = x + _dot(merged.astype(jnp.bfloat16), wo_ref[...])
    if final:
        out = _rms_scale(out) * fg_ref[...]
    o_ref[0] = out


def _merge_out(layer, x, g, ya, yb, yc, yd, w_hi, wb, wo, fg, final):
    bsz, seq, d = x.shape
    t = T_OUT
    const = lambda shape: pl.BlockSpec(shape, lambda b, j: (0,) * len(shape))
    tok = lambda width: pl.BlockSpec((1, t, width), lambda b, j: (b, j, 0))
    first = H_MERGE // H_BLOCK
    return pl.pallas_call(
        functools.partial(_merge_out_kernel, final=final),
        grid=(bsz, seq // t),
        in_specs=[
            tok(d), const(g.shape), tok(BR), tok(BR), tok(BR), tok(BR),
            pl.BlockSpec((1, d, H_BLOCK), lambda b, j: (layer, 0, first)),
            pl.BlockSpec((1, d, H_BLOCK), lambda b, j: (layer, 0, first + 1)),
            const(wb.shape), const(wo.shape), const(fg.shape),
        ],
        out_specs=tok(d),
        out_shape=jax.ShapeDtypeStruct((bsz, seq, d), jnp.float32),
        compiler_params=pltpu.CompilerParams(
            dimension_semantics=("arbitrary", "arbitrary"), vmem_limit_bytes=VMEM_LIMIT),
        name="merge_out_final" if final else "merge_out",
    )(x, g, ya, yb, yc, yd, w_hi, w_hi, wb, wo, fg)


def _weight_prep_kernel(wa_in, wq_in, wh_in, wa_ref, wqk_ref, wvt_ref, whi_ref):
    bf = jnp.bfloat16
    depth = wh_in.shape[1]
    for l in range(depth):
        whi_ref[l] = wh_in[:, l, :].T.astype(bf)

    @pl.when(pl.program_id(0) == 0)
    def _():
        lane = lax.broadcasted_iota(jnp.int32, (wa_in.shape[2], AUG), 1)
        for l in range(depth):
            wa_ref[l] = wa_in[:, l, :].T.astype(bf)
            xq = wq_in[:, l, :]
            wvt_ref[l] = xq[2 * BR:3 * BR].astype(bf)
            x = xq.T
            f_col = x[:, 3 * BR:3 * BR + LANES]
            for h in range(N_HEADS):
                lo, _ = _head_lanes(h)
                in_head = (lane >= lo) & (lane < lo + HEAD_DIM)
                c0 = (h // 2) * LANES
                q_col = x[:, c0:c0 + LANES] * (LOG2E / math.sqrt(HEAD_DIM))
                k_col = x[:, c0 + BR:c0 + BR + LANES]
                rest = jnp.where(lane < F_LANE + N_HEADS, f_col, 0.0) if h == F_HEAD else 0.0
                wqk_ref[l, :, h * AUG:(h + 1) * AUG] = jnp.where(in_head, q_col, rest).astype(bf)
                wqk_ref[l, :, (N_HEADS + h) * AUG:(N_HEADS + h + 1) * AUG] = (
                    jnp.where(in_head, k_col, 0.0).astype(bf))


def _weight_prep(w_in):
    depth, d, cols = w_in.shape
    bf = jnp.bfloat16
    w_t = jnp.transpose(w_in, (2, 0, 1))
    n_hi = (cols - W_HI) // PREP_COLS
    window = lambda n: (pl.Element(n), pl.Element(depth), pl.Element(d))
    return pl.pallas_call(
        _weight_prep_kernel,
        grid=(n_hi,),
        in_specs=[
            pl.BlockSpec(window(W_LO), lambda i: (0, 0, 0)),
            pl.BlockSpec(window(QKVF_COLS), lambda i: (W_LO, 0, 0)),
            pl.BlockSpec(window(PREP_COLS), lambda i: (W_HI + PREP_COLS * i, 0, 0)),
        ],
        out_specs=[pl.BlockSpec((depth, d, W_LO), lambda i: (0, 0, 0)),
                   pl.BlockSpec((depth, d, 2 * N_HEADS * AUG), lambda i: (0, 0, 0)),
                   pl.BlockSpec((depth, BR, d), lambda i: (0, 0, 0)),
                   pl.BlockSpec((depth, d, PREP_COLS), lambda i: (0, 0, i))],
        out_shape=[jax.ShapeDtypeStruct((depth, d, W_LO), bf),
                   jax.ShapeDtypeStruct((depth, d, 2 * N_HEADS * AUG), bf),
                   jax.ShapeDtypeStruct((depth, BR, d), bf),
                   jax.ShapeDtypeStruct((depth, d, cols - W_HI), bf)],
        compiler_params=pltpu.CompilerParams(
            dimension_semantics=("arbitrary",), vmem_limit_bytes=VMEM_LIMIT),
        name="weight_prep",
    )(w_t, w_t, w_t)


def _bias_placement():
    place = np.zeros((LANES, 2 * N_HEADS * AUG), np.float32)
    for h in range(N_HEADS):
        _, a0 = _head_lanes(h)
        qc, kc = h * AUG + a0, (N_HEADS + h) * AUG + a0
        for piece in range(3):
            src = F_LANE + piece * N_HEADS + h
            place[src, qc + piece] = 1.0
            place[src, kc + 3 + piece] = -1.0
            place[ONES_LANE, qc + 3 + piece] = 1.0
            place[ONES_LANE, kc + piece] = 1.0
    return jnp.asarray(place, jnp.bfloat16)


def kernel(x, norm_g, w_in, f_bias, sgu_w, sgu_b, sgu_ln_g, sgu_ln_b, short_conv_w,
           conf_dw_w, conf_dw_b, conf_ln_g, conf_ln_b, w_branch, w_out, final_g):
    depth = norm_g.shape[0]
    bsz, seq, d_model = x.shape
    assert d_model == D_MODEL and w_in.shape[2] == W_HI + H_MERGE + N_BRANCH * D_MODEL
    assert seq % T_MIX == 0 and seq % T_ATT == 0 and seq % T_OUT == 0
    bf = jnp.bfloat16
    fg = final_g.reshape(1, D_MODEL)
    w_bf, wqk, wvt, w_hi = _weight_prep(w_in)
    wb_bf = w_branch.astype(bf)
    wo_bf = w_out.astype(bf)
    place = _bias_placement()
    for layer in range(depth):
        g = norm_g[layer].reshape(1, D_MODEL)
        fb = jnp.zeros((1, LANES), jnp.float32).at[0, F_LANE:F_LANE + N_HEADS].set(f_bias[layer])
        sguw = sgu_w[layer].reshape(N_HEADS * CHUNK, CHUNK)
        sgub = jnp.repeat(sgu_b[layer].T, HEAD_DIM, axis=1)
        aln = jnp.stack([sgu_ln_g[layer], sgu_ln_b[layer]])
        scw = jnp.zeros((SUBLANES, BR), jnp.float32).at[:SHORT_CONV].set(short_conv_w[layer])
        dww = jnp.broadcast_to(conf_dw_w[layer][:, None, :], (CONF_CONV, SUBLANES, BR))
        dvec = jnp.stack([conf_dw_b[layer], conf_ln_g[layer], conf_ln_b[layer]])
        ya, yc, yd, gb, q_aug, k_aug, vt = _mixer_in(
            layer, x, g, w_bf, wqk, w_hi, wvt, fb, sguw, sgub, aln, scw, dww, dvec, place)
        yb = _fox_attn(q_aug, k_aug, vt, gb)
        x = _merge_out(layer, x, g, ya, yb, yc, yd, w_hi, wb_bf[layer], wo_bf[layer], fg,
                       final=(layer == depth - 1))
    return x
```

```python
import functools
import math

import jax
import jax.numpy as jnp
import numpy as np
from jax import lax
from jax.experimental import pallas as pl
from jax.experimental.pallas import tpu as pltpu

D_MODEL = 1024
N_BRANCH = 4
BR = D_MODEL // N_BRANCH
HEAD_DIM = 64
N_HEADS = BR // HEAD_DIM
CHUNK = 128
SHORT_CONV = 3
CONF_CONV = 31
EPS = 1e-6

LANES = 128
SUBLANES = 8
AUG = LANES
CONF_HALO = 32
SHORT_HALO = SUBLANES
CONV_ROWS = 64

T_MIX = 1024
T_ATT = 512
V_ROWS = HEAD_DIM + 16
LOG2E = math.log2(math.e)
SCORE_LOOKAHEAD = 2
T_OUT = 1024
PREP_COLS = 512
QKVF_COLS = 896
VMEM_LIMIT = 56 * 1024 * 1024

W_LO = 768
W_HI = 1540
H_BG, H_C, H_CG, H_D, H_DG, H_MERGE = 0, 256, 1024, 1280, 1792, 2048
H_BLOCK = 2048
F_HEAD, F_LANE = 1, 0
ONES_LANE = F_LANE + 3 * N_HEADS

NEG = -1e30

_NT = (((1,), (1,)), ((), ()))


def _dot(a, b):
    return jnp.dot(a, b, preferred_element_type=jnp.float32)


def _sigmoid(x):
    return 0.5 + 0.5 * jnp.tanh(0.5 * x)


def _silu(x):
    h = 0.5 * x
    return h + h * jnp.tanh(h)


def _gelu_tanh(x):
    c = math.sqrt(2.0 / math.pi)
    return 0.5 * x * (1.0 + jnp.tanh(c * (x + 0.044715 * (x * x * x))))


def _log_sigmoid(x):
    return jnp.minimum(x, 0.0) - jnp.log(1.0 + jnp.exp(-jnp.abs(x)))


def _layernorm(x, g, b):
    mu = jnp.mean(x, axis=-1, keepdims=True)
    xc = x - mu
    var = jnp.mean(xc * xc, axis=-1, keepdims=True)
    return xc * lax.rsqrt(var + EPS) * g + b


def _rms_scale(x):
    return x * lax.rsqrt(jnp.mean(x * x, axis=-1, keepdims=True) + EPS)


def _split3(x):
    hi = x.astype(jnp.bfloat16).astype(jnp.float32)
    r = x - hi
    mid = r.astype(jnp.bfloat16).astype(jnp.float32)
    lo = r - mid
    return hi, mid, lo


def _head_lanes(h):
    return (0, HEAD_DIM) if h % 2 == 0 else (HEAD_DIM, SUBLANES)


def _mixer_in_kernel(x_ref, g_ref, wa_ref, wqk_ref, wh_ref, wvt_ref, fb_ref, sguw_ref, sgub_ref,
                     aln_ref, scw_ref, dww_ref, dvec_ref, place_ref,
                     ya_ref, yc_ref, yd_ref, gb_ref, q_ref, k_ref, vt_ref,
                     zbuf, hbuf, cum_carry):
    t = x_ref.shape[1]
    j = pl.program_id(1)

    @pl.when(j == 0)
    def _():
        zbuf[0:SHORT_HALO, :] = jnp.zeros((SHORT_HALO, BR), jnp.float32)
        hbuf[0:CONF_HALO, :] = jnp.zeros((CONF_HALO, BR), jnp.float32)
        cum_carry[...] = jnp.zeros_like(cum_carry)

    x = x_ref[0]
    hb = (_rms_scale(x) * g_ref[...]).astype(jnp.bfloat16)


    glu = _dot(hb, wh_ref[0, :, H_D:H_DG])
    gd_lin = _dot(hb, wh_ref[0, :, H_DG:H_MERGE])
    hbuf[CONF_HALO:CONF_HALO + t, :] = glu[:, :BR] * _sigmoid(glu[:, BR:])

    def conf_rows(r0):
        dconv = jnp.broadcast_to(dvec_ref[0:1, :], (CONV_ROWS, BR))
        win = hbuf[r0:r0 + CONF_HALO + CONV_ROWS, :]
        for b in range(SUBLANES):
            wb = win if b == 0 else pltpu.roll(win, b, 0)
            for a in range(CONF_HALO // SUBLANES):
                s = SUBLANES * a + b
                if s < CONF_CONV:
                    kk = CONF_CONV - 1 - s
                    off = CONF_HALO - SUBLANES * a
                    wk = jnp.concatenate([dww_ref[kk]] * (CONV_ROWS // SUBLANES), axis=0)
                    dconv = dconv + wb[off:off + CONV_ROWS, :] * wk
        yd_ref[0, r0:r0 + CONV_ROWS, :] = (
            _silu(_layernorm(dconv, dvec_ref[1:2, :], dvec_ref[2:3, :]))
            * _silu(gd_lin[r0:r0 + CONV_ROWS])).astype(yd_ref.dtype)

    qa = _dot(hb, wqk_ref[0, :, 0:N_HEADS * AUG])
    ka = _dot(hb, wqk_ref[0, :, N_HEADS * AUG:])
    for r0 in range(0, t // 2, CONV_ROWS):
        conf_rows(r0)

    lf = _log_sigmoid(qa[:, F_HEAD * AUG:(F_HEAD + 1) * AUG] + fb_ref[...])
    r2 = lax.broadcasted_iota(jnp.int32, (CHUNK, CHUNK), 0)
    c2 = lax.broadcasted_iota(jnp.int32, (CHUNK, CHUNK), 1)
    tri = jnp.where(r2 >= c2, 1.0, 0.0).astype(jnp.bfloat16)
    lane_c = lax.broadcasted_iota(jnp.int32, (CHUNK, LANES), 1)
    grp = (lane_c - F_LANE) // N_HEADS
    within = []
    for c in range(t // CHUNK):
        hi, mid, lo = _split3(lf[c * CHUNK:(c + 1) * CHUNK])
        packed = jnp.where(grp == 0, hi,
                           jnp.where(grp == 1, pltpu.roll(mid, N_HEADS, 1),
                                     jnp.where(grp == 2, pltpu.roll(lo, 2 * N_HEADS, 1), 0.0)))
        r = _dot(tri, packed.astype(jnp.bfloat16))
        within.append(r + pltpu.roll(r, LANES - N_HEADS, 1)
                      + pltpu.roll(r, LANES - 2 * N_HEADS, 1))

    uv = _dot(hb, wa_ref[0, :, 0:2 * BR])
    ga_lin = _dot(hb, wa_ref[0, :, 2 * BR:W_LO])
    for r0 in range(t // 2, t, CONV_ROWS):
        conf_rows(r0)
    hbuf[0:CONF_HALO, :] = hbuf[t:t + CONF_HALO, :]

    cin = _dot(hb, wh_ref[0, :, H_C:H_CG])
    gc_lin = _dot(hb, wh_ref[0, :, H_CG:H_D])

    carry = cum_carry[...]
    cums = []
    for cc in within:
        cums.append(cc + carry)
        carry = carry + cc[CHUNK - 1:CHUNK, :]
    cum_carry[...] = carry
    cum = jnp.concatenate(cums, axis=0) * LOG2E

    hi, mid, lo = _split3(cum)
    lane_t = lax.broadcasted_iota(jnp.int32, (t, LANES), 1)
    grp_t = (lane_t - F_LANE) // N_HEADS
    packed = jnp.where(grp_t == 0, hi,
                       jnp.where(grp_t == 1, pltpu.roll(mid, N_HEADS, 1),
                                 jnp.where(grp_t == 2, pltpu.roll(lo, 2 * N_HEADS, 1),
                                           jnp.where(lane_t == ONES_LANE, 1.0, 0.0))))
    bias = _dot(packed.astype(jnp.bfloat16), place_ref[...])
    qa = qa + bias[:, :N_HEADS * AUG]
    ka = ka + bias[:, N_HEADS * AUG:]
    for h in range(N_HEADS):
        q_ref[0, h] = qa[:, h * AUG:(h + 1) * AUG].astype(q_ref.dtype)
        k_ref[0, h] = ka[:, h * AUG:(h + 1) * AUG].astype(k_ref.dtype)

    gb_lin = _dot(hb, wh_ref[0, :, H_BG:H_C])
    vt = lax.dot_general(wvt_ref[0], hb, _NT, preferred_element_type=jnp.float32)

    gel = _gelu_tanh(uv)
    u = gel[:, :BR]
    v = _layernorm(gel[:, BR:], aln_ref[0:1, :], aln_ref[1:2, :])
    rows = lax.broadcasted_iota(jnp.int32, (N_HEADS * CHUNK, CHUNK), 0)
    cols = lax.broadcasted_iota(jnp.int32, (N_HEADS * CHUNK, CHUNK), 1)
    wst = jnp.where((rows % CHUNK) >= cols, sguw_ref[...], 0.0).astype(jnp.bfloat16)
    malls = [_dot(wst, v[c * CHUNK:(c + 1) * CHUNK].astype(jnp.bfloat16))
             for c in range(t // CHUNK)]

    zbuf[SHORT_HALO:SHORT_HALO + t, :] = cin[:, BR:2 * BR] * cin[:, 2 * BR:]
    conv = jnp.zeros((t, BR), jnp.float32)
    for kk in range(SHORT_CONV):
        s = SHORT_CONV - 1 - kk
        conv = conv + zbuf[SHORT_HALO - s:SHORT_HALO - s + t, :] * scw_ref[kk:kk + 1, :]
    yc_ref[0] = (cin[:, :BR] * conv * _silu(gc_lin)).astype(yc_ref.dtype)
    zbuf[0:SHORT_HALO, :] = zbuf[t:t + SHORT_HALO, :]

    gb_ref[0] = _silu(gb_lin).astype(gb_ref.dtype)
    vt_ref[0] = vt.astype(vt_ref.dtype)

    lane_head = lax.broadcasted_iota(jnp.int32, (CHUNK, BR), 1) // HEAD_DIM
    ga = _silu(ga_lin)
    for c in range(t // CHUNK):
        sl = slice(c * CHUNK, (c + 1) * CHUNK)
        mixed = sgub_ref[...]
        for h in range(N_HEADS):
            mixed = mixed + jnp.where(lane_head == h, malls[c][h * CHUNK:(h + 1) * CHUNK], 0.0)
        ya_ref[0, sl, :] = (u[sl] * mixed * ga[sl]).astype(ya_ref.dtype)


def _mixer_in(layer, x, g, w_bf, wqk, w_hi, wvt, fb, sguw, sgub, aln, scw, dww, dvec, place):
    bsz, seq, d = x.shape
    t = T_MIX
    nblk = seq // t
    bf = jnp.bfloat16
    const = lambda shape: pl.BlockSpec(shape, lambda b, j: (0,) * len(shape))
    tok = lambda width: pl.BlockSpec((1, t, width), lambda b, j: (b, j, 0))
    return pl.pallas_call(
        _mixer_in_kernel,
        grid=(bsz, nblk),
        in_specs=[
            tok(d), const(g.shape),
            pl.BlockSpec((1, d, W_LO), lambda b, j: (layer, 0, 0)),
            pl.BlockSpec((1, d, 2 * N_HEADS * AUG), lambda b, j: (layer, 0, 0)),
            pl.BlockSpec((1, d, H_BLOCK), lambda b, j: (layer, 0, 0)),
            pl.BlockSpec((1, BR, d), lambda b, j: (layer, 0, 0)), const(fb.shape),
            const(sguw.shape), const(sgub.shape), const(aln.shape), const(scw.shape),
            const(dww.shape), const(dvec.shape), const(place.shape),
        ],
        out_specs=[
            tok(BR), tok(BR), tok(BR), tok(BR),
            pl.BlockSpec((1, N_HEADS, t, AUG), lambda b, j: (b, 0, j, 0)),
            pl.BlockSpec((1, N_HEADS, t, AUG), lambda b, j: (b, 0, j, 0)),
            pl.BlockSpec((1, BR, t), lambda b, j: (b, 0, j)),
        ],
        out_shape=[
            jax.ShapeDtypeStruct((bsz, seq, BR), bf),
            jax.ShapeDtypeStruct((bsz, seq, BR), bf),
            jax.ShapeDtypeStruct((bsz, seq, BR), bf),
            jax.ShapeDtypeStruct((bsz, seq, BR), bf),
            jax.ShapeDtypeStruct((bsz, N_HEADS, seq, AUG), bf),
            jax.ShapeDtypeStruct((bsz, N_HEADS, seq, AUG), bf),
            jax.ShapeDtypeStruct((bsz, BR, seq), bf),
        ],
        scratch_shapes=[
            pltpu.VMEM((SHORT_HALO + t, BR), jnp.float32),
            pltpu.VMEM((CONF_HALO + t, BR), jnp.float32),
            pltpu.VMEM((1, LANES), jnp.float32),
        ],
        compiler_params=pltpu.CompilerParams(
            dimension_semantics=("arbitrary", "arbitrary"), vmem_limit_bytes=VMEM_LIMIT),
        name="mixer_in",
    )(x, g, w_bf, wqk, w_hi, wvt, fb, sguw, sgub, aln, scw, dww, dvec, place)


def _fox_attn_kernel(q_ref, k_ref, vt_ref, gb_ref, o_ref, m_sc, acc_sc, s_buf):
    tq = q_ref.shape[2]
    tk = tq
    qi = pl.program_id(1)
    m_sc[...] = jnp.full(m_sc.shape, NEG, jnp.float32)
    acc_sc[...] = jnp.zeros(acc_sc.shape, jnp.float32)
    ones_rows = jnp.ones((V_ROWS - HEAD_DIM, tk), jnp.bfloat16)

    def scores(h, kt, slot):
        ks = pl.multiple_of(kt * tk, tk)
        k = k_ref[0, h, pl.ds(ks, tk), :]
        s_buf[slot, :, 0:tq] = lax.dot_general(k, q_ref[0, h], _NT,
                                               preferred_element_type=jnp.float32)
        return slot

    def causal(s):
        kpos = lax.broadcasted_iota(jnp.int32, s.shape, 0)
        qpos = lax.broadcasted_iota(jnp.int32, s.shape, 1)
        return jnp.where(kpos <= qpos, s, NEG)

    def accumulate(h, kt, slot):
        s = s_buf[slot, :, 0:tq]
        ks = pl.multiple_of(kt * tk, tk)
        m_old = m_sc[h]
        m_new = jnp.maximum(m_old, jnp.max(s, axis=0, keepdims=True))
        p = jnp.exp2(s - m_new).astype(jnp.bfloat16)
        vt = jnp.concatenate(
            [vt_ref[0, h * HEAD_DIM:(h + 1) * HEAD_DIM, pl.ds(ks, tk)], ones_rows], axis=0)
        acc_sc[h] = jnp.exp2(m_old - m_new) * acc_sc[h] + _dot(vt, p)
        m_sc[h] = m_new

    def step(kt, parity):
        here, there = (0, N_HEADS) if parity == 0 else (N_HEADS, 0)
        todo = [(h, kt, h) for h in range(1, N_HEADS)] + [(0, kt + 1, there)]
        ready = [here]
        for h in range(N_HEADS):
            while todo and len(ready) < 1 + SCORE_LOOKAHEAD:
                ready.append(scores(*todo.pop(0)))
            accumulate(h, kt, ready.pop(0))

    half = tk // 2

    def diag_scores(h, ks, slot0):
        if h == 0:
            return slot0
        k_top = k_ref[0, h, pl.ds(ks, half), :]
        k_bot = k_ref[0, h, pl.ds(pl.multiple_of(ks + half, half), half), :]
        s_buf[h, 0:half, 0:tq] = lax.dot_general(k_top, q_ref[0, h], _NT,
                                                 preferred_element_type=jnp.float32)
        s_buf[h, half:tk, half:tq] = lax.dot_general(k_bot, q_ref[0, h, half:, :], _NT,
                                                     preferred_element_type=jnp.float32)
        return h

    def diag_accumulate(h, ks, slot):
        s_tl = causal(s_buf[slot, 0:half, 0:half])
        s_tr = s_buf[slot, 0:half, half:tq]
        s_br = causal(s_buf[slot, half:tk, half:tq])
        m_old = m_sc[h]
        m_l = jnp.maximum(m_old[:, :half], jnp.max(s_tl, axis=0, keepdims=True))
        m_r = jnp.maximum(
            m_old[:, half:],
            jnp.maximum(jnp.max(s_tr, axis=0, keepdims=True), jnp.max(s_br, axis=0, keepdims=True)))
        p_tl = jnp.exp2(s_tl - m_l).astype(jnp.bfloat16)
        p_tr = jnp.exp2(s_tr - m_r).astype(jnp.bfloat16)
        p_br = jnp.exp2(s_br - m_r).astype(jnp.bfloat16)
        rows = slice(h * HEAD_DIM, (h + 1) * HEAD_DIM)
        vt_top = jnp.concatenate([vt_ref[0, rows, pl.ds(ks, half)], ones_rows[:, :half]], axis=0)
        vt_bot = jnp.concatenate(
            [vt_ref[0, rows, pl.ds(pl.multiple_of(ks + half, half), half)], ones_rows[:, :half]],
            axis=0)
        acc = acc_sc[h]
        acc_l = jnp.exp2(m_old[:, :half] - m_l) * acc[:, :half] + _dot(vt_top, p_tl)
        acc_r = (jnp.exp2(m_old[:, half:] - m_r) * acc[:, half:]
                 + _dot(vt_top, p_tr) + _dot(vt_bot, p_br))
        acc_sc[h] = jnp.concatenate([acc_l, acc_r], axis=1)

    def diag_step(kt, slot0):
        ks = pl.multiple_of(kt * tk, tk)
        todo = list(range(1, N_HEADS))
        ready = [diag_scores(0, ks, slot0)]
        for h in range(N_HEADS):
            while todo and len(ready) < 1 + SCORE_LOOKAHEAD:
                ready.append(diag_scores(todo.pop(0), ks, slot0))
            diag_accumulate(h, ks, ready.pop(0))

    def body2(j, carry):
        step(2 * j, 0)
        step(2 * j + 1, 1)
        return carry

    scores(0, 0, 0)

    @pl.when(qi == 0)
    def _():
        diag_step(qi, 0)

    @pl.when(qi % 2 == 1)
    def _():
        lax.fori_loop(0, (qi - 1) // 2, body2, 0)
        step(qi - 1, 0)
        diag_step(qi, N_HEADS)

    @pl.when((qi % 2 == 0) & (qi > 0))
    def _():
        lax.fori_loop(0, (qi - 2) // 2, body2, 0)
        step(qi - 2, 0)
        step(qi - 1, 1)
        diag_step(qi, 0)
    outs = []
    for h in range(N_HEADS):
        acc = acc_sc[h]
        outs.append(acc[:HEAD_DIM] / acc[HEAD_DIM:HEAD_DIM + 1])
    o = jnp.concatenate(outs, axis=0).T
    o_ref[0] = (o * gb_ref[0].astype(jnp.float32)).astype(o_ref.dtype)


def _fox_attn(q_aug, k_aug, vt, gb):
    bsz, nh, seq, aug = q_aug.shape
    tq = T_ATT
    return pl.pallas_call(
        _fox_attn_kernel,
        grid=(bsz, seq // tq),
        in_specs=[
            pl.BlockSpec((1, nh, tq, aug), lambda b, i: (b, 0, i, 0)),
            pl.BlockSpec((1, nh, seq, aug), lambda b, i: (b, 0, 0, 0)),
            pl.BlockSpec((1, BR, seq), lambda b, i: (b, 0, 0)),
            pl.BlockSpec((1, tq, BR), lambda b, i: (b, i, 0)),
        ],
        out_specs=pl.BlockSpec((1, tq, BR), lambda b, i: (b, i, 0)),
        out_shape=jax.ShapeDtypeStruct((bsz, seq, BR), jnp.bfloat16),
        scratch_shapes=[
            pltpu.VMEM((nh, 1, tq), jnp.float32),
            pltpu.VMEM((nh, V_ROWS, tq), jnp.float32),
            pltpu.VMEM((nh + 1, tq, tq + LANES), jnp.float32),
        ],
        compiler_params=pltpu.CompilerParams(
            dimension_semantics=("arbitrary", "arbitrary"), vmem_limit_bytes=VMEM_LIMIT),
        name="fox_attn",
    )(q_aug, k_aug, vt, gb)


def _merge_out_kernel(x_ref, g_ref, ya_ref, yb_ref, yc_ref, yd_ref,
                      wm0_ref, wm1_ref, wb_ref, wo_ref, fg_ref, o_ref, *, final):
    x = x_ref[0]
    hb = (_rms_scale(x) * g_ref[...]).astype(jnp.bfloat16)
    ys = (ya_ref[0], yb_ref[0], yc_ref[0], yd_ref[0])
    per_ref = H_BLOCK // D_MODEL
    merged = jnp.zeros(x.shape, jnp.float32)
    for n in range(N_BRANCH):
        wm_ref = (wm0_ref, wm1_ref)[n // per_ref]
        c0 = (n % per_ref) * D_MODEL
        gate = _sigmoid(_dot(hb, wm_ref[0, :, c0:c0 + D_MODEL]))
        merged = merged + gate * _dot(ys[n], wb_ref[n])
    out = x + _dot(merged.astype(jnp.bfloat16), wo_ref[...])
    if final:
        out = _rms_scale(out) * fg_ref[...]
    o_ref[0] = out


def _merge_out(layer, x, g, ya, yb, yc, yd, w_hi, wb, wo, fg, final):
    bsz, seq, d = x.shape
    t = T_OUT
    const = lambda shape: pl.BlockSpec(shape, lambda b, j: (0,) * len(shape))
    tok = lambda width: pl.BlockSpec((1, t, width), lambda b, j: (b, j, 0))
    first = H_MERGE // H_BLOCK
    return pl.pallas_call(
        functools.partial(_merge_out_kernel, final=final),
        grid=(bsz, seq // t),
        in_specs=[
            tok(d), const(g.shape), tok(BR), tok(BR), tok(BR), tok(BR),
            pl.BlockSpec((1, d, H_BLOCK), lambda b, j: (layer, 0, first)),
            pl.BlockSpec((1, d, H_BLOCK), lambda b, j: (layer, 0, first + 1)),
            const(wb.shape), const(wo.shape), const(fg.shape),
        ],
        out_specs=tok(d),
        out_shape=jax.ShapeDtypeStruct((bsz, seq, d), jnp.float32),
        compiler_params=pltpu.CompilerParams(
            dimension_semantics=("arbitrary", "arbitrary"), vmem_limit_bytes=VMEM_LIMIT),
        name="merge_out_final" if final else "merge_out",
    )(x, g, ya, yb, yc, yd, w_hi, w_hi, wb, wo, fg)


def _weight_prep_kernel(wa_in, wq_in, wh_in, wa_ref, wqk_ref, wvt_ref, whi_ref):
    bf = jnp.bfloat16
    depth = wh_in.shape[1]
    for l in range(depth):
        whi_ref[l] = wh_in[:, l, :].T.astype(bf)

    @pl.when(pl.program_id(0) == 0)
    def _():
        lane = lax.broadcasted_iota(jnp.int32, (wa_in.shape[2], AUG), 1)
        for l in range(depth):
            wa_ref[l] = wa_in[:, l, :].T.astype(bf)
            xq = wq_in[:, l, :]
            wvt_ref[l] = xq[2 * BR:3 * BR].astype(bf)
            x = xq.T
            f_col = x[:, 3 * BR:3 * BR + LANES]
            for h in range(N_HEADS):
                lo, _ = _head_lanes(h)
                in_head = (lane >= lo) & (lane < lo + HEAD_DIM)
                c0 = (h // 2) * LANES
                q_col = x[:, c0:c0 + LANES] * (LOG2E / math.sqrt(HEAD_DIM))
                k_col = x[:, c0 + BR:c0 + BR + LANES]
                rest = jnp.where(lane < F_LANE + N_HEADS, f_col, 0.0) if h == F_HEAD else 0.0
                wqk_ref[l, :, h * AUG:(h + 1) * AUG] = jnp.where(in_head, q_col, rest).astype(bf)
                wqk_ref[l, :, (N_HEADS + h) * AUG:(N_HEADS + h + 1) * AUG] = (
                    jnp.where(in_head, k_col, 0.0).astype(bf))


def _weight_prep(w_in):
    depth, d, cols = w_in.shape
    bf = jnp.bfloat16
    w_t = jnp.transpose(w_in, (2, 0, 1))
    n_hi = (cols - W_HI) // PREP_COLS
    window = lambda n: (pl.Element(n), pl.Element(depth), pl.Element(d))
    return pl.pallas_call(
        _weight_prep_kernel,
        grid=(n_hi,),
        in_specs=[
            pl.BlockSpec(window(W_LO), lambda i: (0, 0, 0)),
            pl.BlockSpec(window(QKVF_COLS), lambda i: (W_LO, 0, 0)),
            pl.BlockSpec(window(PREP_COLS), lambda i: (W_HI + PREP_COLS * i, 0, 0)),
        ],
        out_specs=[pl.BlockSpec((depth, d, W_LO), lambda i: (0, 0, 0)),
                   pl.BlockSpec((depth, d, 2 * N_HEADS * AUG), lambda i: (0, 0, 0)),
                   pl.BlockSpec((depth, BR, d), lambda i: (0, 0, 0)),
                   pl.BlockSpec((depth, d, PREP_COLS), lambda i: (0, 0, i))],
        out_shape=[jax.ShapeDtypeStruct((depth, d, W_LO), bf),
                   jax.ShapeDtypeStruct((depth, d, 2 * N_HEADS * AUG), bf),
                   jax.ShapeDtypeStruct((depth, BR, d), bf),
                   jax.ShapeDtypeStruct((depth, d, cols - W_HI), bf)],
        compiler_params=pltpu.CompilerParams(
            dimension_semantics=("arbitrary",), vmem_limit_bytes=VMEM_LIMIT),
        name="weight_prep",
    )(w_t, w_t, w_t)


def _bias_placement():
    place = np.zeros((LANES, 2 * N_HEADS * AUG), np.float32)
    for h in range(N_HEADS):
        _, a0 = _head_lanes(h)
        qc, kc = h * AUG + a0, (N_HEADS + h) * AUG + a0
        for piece in range(3):
            src = F_LANE + piece * N_HEADS + h
            place[src, qc + piece] = 1.0
            place[src, kc + 3 + piece] = -1.0
            place[ONES_LANE, qc + 3 + piece] = 1.0
            place[ONES_LANE, kc + piece] = 1.0
    return jnp.asarray(place, jnp.bfloat16)


def kernel(x, norm_g, w_in, f_bias, sgu_w, sgu_b, sgu_ln_g, sgu_ln_b, short_conv_w,
           conf_dw_w, conf_dw_b, conf_ln_g, conf_ln_b, w_branch, w_out, final_g):
    depth = norm_g.shape[0]
    bsz, seq, d_model = x.shape
    assert d_model == D_MODEL and w_in.shape[2] == W_HI + H_MERGE + N_BRANCH * D_MODEL
    assert seq % T_MIX == 0 and seq % T_ATT == 0 and seq % T_OUT == 0
    bf = jnp.bfloat16
    fg = final_g.reshape(1, D_MODEL)
    w_bf, wqk, wvt, w_hi = _weight_prep(w_in)
    wb_bf = w_branch.astype(bf)
    wo_bf = w_out.astype(bf)
    place = _bias_placement()
    for layer in range(depth):
        g = norm_g[layer].reshape(1, D_MODEL)
        fb = jnp.zeros((1, LANES), jnp.float32).at[0, F_LANE:F_LANE + N_HEADS].set(f_bias[layer])
        sguw = sgu_w[layer].reshape(N_HEADS * CHUNK, CHUNK)
        sgub = jnp.repeat(sgu_b[layer].T, HEAD_DIM, axis=1)
        aln = jnp.stack([sgu_ln_g[layer], sgu_ln_b[layer]])
        scw = jnp.zeros((SUBLANES, BR), jnp.float32).at[:SHORT_CONV].set(short_conv_w[layer])
        dww = jnp.broadcast_to(conf_dw_w[layer][:, None, :], (CONF_CONV, SUBLANES, BR))
        dvec = jnp.stack([conf_dw_b[layer], conf_ln_g[layer], conf_ln_b[layer]])
        ya, yc, yd, gb, q_aug, k_aug, vt = _mixer_in(
            layer, x, g, w_bf, wqk, w_hi, wvt, fb, sguw, sgub, aln, scw, dww, dvec, place)
        yb = _fox_attn(q_aug, k_aug, vt, gb)
        x = _merge_out(layer, x, g, ya, yb, yc, yd, w_hi, wb_bf[layer], wo_bf[layer], fg,
                       final=(layer == depth - 1))
    return x
```

```python
import functools
import math

import jax
import jax.numpy as jnp
import numpy as np
from jax import lax
from jax.experimental import pallas as pl
from jax.experimental.pallas import tpu as pltpu

D_MODEL = 1024
N_BRANCH = 4
BR = D_MODEL // N_BRANCH
HEAD_DIM = 64
N_HEADS = BR // HEAD_DIM
CHUNK = 128
SHORT_CONV = 3
CONF_CONV = 31
EPS = 1e-6

LANES = 128
SUBLANES = 8
AUG = LANES
CONF_HALO = 32
SHORT_HALO = SUBLANES
CONV_ROWS = 32

T_MIX = 1024
T_ATT = 512
V_ROWS = HEAD_DIM + 16
LOG2E = math.log2(math.e)
SCORE_LOOKAHEAD = 2
T_OUT = 1024
PREP_COLS = 512
QKVF_COLS = 896
VMEM_LIMIT = 56 * 1024 * 1024

W_LO = 768
W_HI = 1540
H_BG, H_C, H_CG, H_D, H_DG, H_MERGE = 0, 256, 1024, 1280, 1792, 2048
H_BLOCK = 2048
F_HEAD, F_LANE = 1, 0
ONES_LANE = F_LANE + 3 * N_HEADS

NEG = -1e30

_NT = (((1,), (1,)), ((), ()))


def _dot(a, b):
    return jnp.dot(a, b, preferred_element_type=jnp.float32)


def _sigmoid(x):
    return 0.5 + 0.5 * jnp.tanh(0.5 * x)


def _silu(x):
    h = 0.5 * x
    return h + h * jnp.tanh(h)


def _gelu_tanh(x):
    c = math.sqrt(2.0 / math.pi)
    return 0.5 * x * (1.0 + jnp.tanh(c * (x + 0.044715 * (x * x * x))))


def _log_sigmoid(x):
    return jnp.minimum(x, 0.0) - jnp.log(1.0 + jnp.exp(-jnp.abs(x)))


def _layernorm(x, g, b):
    mu = jnp.mean(x, axis=-1, keepdims=True)
    xc = x - mu
    var = jnp.mean(xc * xc, axis=-1, keepdims=True)
    return xc * lax.rsqrt(var + EPS) * g + b


def _rms_scale(x):
    return x * lax.rsqrt(jnp.mean(x * x, axis=-1, keepdims=True) + EPS)


def _split3(x):
    hi = x.astype(jnp.bfloat16).astype(jnp.float32)
    r = x - hi
    mid = r.astype(jnp.bfloat16).astype(jnp.float32)
    lo = r - mid
    return hi, mid, lo


def _head_lanes(h):
    return (0, HEAD_DIM) if h % 2 == 0 else (HEAD_DIM, SUBLANES)


def _mixer_in_kernel(x_ref, g_ref, wa_ref, wqk_ref, wh_ref, wvt_ref, fb_ref, sguw_ref, sgub_ref,
                     aln_ref, scw_ref, dww_ref, dvec_ref, place_ref,
                     ya_ref, yc_ref, yd_ref, gb_ref, q_ref, k_ref, vt_ref,
                     zbuf, hbuf, cum_carry):
    t = x_ref.shape[1]
    j = pl.program_id(1)

    @pl.when(j == 0)
    def _():
        zbuf[0:SHORT_HALO, :] = jnp.zeros((SHORT_HALO, BR), jnp.float32)
        hbuf[0:CONF_HALO, :] = jnp.zeros((CONF_HALO, BR), jnp.float32)
        cum_carry[...] = jnp.zeros_like(cum_carry)

    x = x_ref[0]
    hb = (_rms_scale(x) * g_ref[...]).astype(jnp.bfloat16)


    glu = _dot(hb, wh_ref[0, :, H_D:H_DG])
    gd_lin = _dot(hb, wh_ref[0, :, H_DG:H_MERGE])
    hbuf[CONF_HALO:CONF_HALO + t, :] = glu[:, :BR] * _sigmoid(glu[:, BR:])

    def conf_rows(r0):
        dconv = jnp.broadcast_to(dvec_ref[0:1, :], (CONV_ROWS, BR))
        win = hbuf[r0:r0 + CONF_HALO + CONV_ROWS, :]
        for b in range(SUBLANES):
            wb = win if b == 0 else pltpu.roll(win, b, 0)
            for a in range(CONF_HALO // SUBLANES):
                s = SUBLANES * a + b
                if s < CONF_CONV:
                    kk = CONF_CONV - 1 - s
                    off = CONF_HALO - SUBLANES * a
                    wk = jnp.concatenate([dww_ref[kk]] * (CONV_ROWS // SUBLANES), axis=0)
                    dconv = dconv + wb[off:off + CONV_ROWS, :] * wk
        yd_ref[0, r0:r0 + CONV_ROWS, :] = (
            _silu(_layernorm(dconv, dvec_ref[1:2, :], dvec_ref[2:3, :]))
            * _silu(gd_lin[r0:r0 + CONV_ROWS])).astype(yd_ref.dtype)

    qa = _dot(hb, wqk_ref[0, :, 0:N_HEADS * AUG])
    ka = _dot(hb, wqk_ref[0, :, N_HEADS * AUG:])
    for r0 in range(0, t // 4, CONV_ROWS):
        conf_rows(r0)
    uv = _dot(hb, wa_ref[0, :, 0:2 * BR])
    ga_lin = _dot(hb, wa_ref[0, :, 2 * BR:W_LO])
    for r0 in range(t // 4, t // 2, CONV_ROWS):
        conf_rows(r0)

    lf = _log_sigmoid(qa[:, F_HEAD * AUG:(F_HEAD + 1) * AUG] + fb_ref[...])
    r2 = lax.broadcasted_iota(jnp.int32, (CHUNK, CHUNK), 0)
    c2 = lax.broadcasted_iota(jnp.int32, (CHUNK, CHUNK), 1)
    tri = jnp.where(r2 >= c2, 1.0, 0.0).astype(jnp.bfloat16)
    lane_c = lax.broadcasted_iota(jnp.int32, (CHUNK, LANES), 1)
    grp = (lane_c - F_LANE) // N_HEADS
    within = []
    for c in range(t // CHUNK):
        hi, mid, lo = _split3(lf[c * CHUNK:(c + 1) * CHUNK])
        packed = jnp.where(grp == 0, hi,
                           jnp.where(grp == 1, pltpu.roll(mid, N_HEADS, 1),
                                     jnp.where(grp == 2, pltpu.roll(lo, 2 * N_HEADS, 1), 0.0)))
        r = _dot(tri, packed.astype(jnp.bfloat16))
        within.append(r + pltpu.roll(r, LANES - N_HEADS, 1)
                      + pltpu.roll(r, LANES - 2 * N_HEADS, 1))

    cin = _dot(hb, wh_ref[0, :, H_C:H_CG])
    gc_lin = _dot(hb, wh_ref[0, :, H_CG:H_D])
    for r0 in range(t // 2, 3 * t // 4, CONV_ROWS):
        conf_rows(r0)
    gb_lin = _dot(hb, wh_ref[0, :, H_BG:H_C])
    vt = lax.dot_general(wvt_ref[0], hb, _NT, preferred_element_type=jnp.float32)
    for r0 in range(3 * t // 4, t, CONV_ROWS):
        conf_rows(r0)
    hbuf[0:CONF_HALO, :] = hbuf[t:t + CONF_HALO, :]

    carry = cum_carry[...]
    cums = []
    for cc in within:
        cums.append(cc + carry)
        carry = carry + cc[CHUNK - 1:CHUNK, :]
    cum_carry[...] = carry
    cum = jnp.concatenate(cums, axis=0) * LOG2E

    hi, mid, lo = _split3(cum)
    lane_t = lax.broadcasted_iota(jnp.int32, (t, LANES), 1)
    grp_t = (lane_t - F_LANE) // N_HEADS
    packed = jnp.where(grp_t == 0, hi,
                       jnp.where(grp_t == 1, pltpu.roll(mid, N_HEADS, 1),
                                 jnp.where(grp_t == 2, pltpu.roll(lo, 2 * N_HEADS, 1),
                                           jnp.where(lane_t == ONES_LANE, 1.0, 0.0))))
    bias = _dot(packed.astype(jnp.bfloat16), place_ref[...])
    qa = qa + bias[:, :N_HEADS * AUG]
    ka = ka + bias[:, N_HEADS * AUG:]
    for h in range(N_HEADS):
        q_ref[0, h] = qa[:, h * AUG:(h + 1) * AUG].astype(q_ref.dtype)
        k_ref[0, h] = ka[:, h * AUG:(h + 1) * AUG].astype(k_ref.dtype)

    gel = _gelu_tanh(uv)
    u = gel[:, :BR]
    v = _layernorm(gel[:, BR:], aln_ref[0:1, :], aln_ref[1:2, :])
    rows = lax.broadcasted_iota(jnp.int32, (N_HEADS * CHUNK, CHUNK), 0)
    cols = lax.broadcasted_iota(jnp.int32, (N_HEADS * CHUNK, CHUNK), 1)
    wst = jnp.where((rows % CHUNK) >= cols, sguw_ref[...], 0.0).astype(jnp.bfloat16)
    malls = [_dot(wst, v[c * CHUNK:(c + 1) * CHUNK].astype(jnp.bfloat16))
             for c in range(t // CHUNK)]

    zbuf[SHORT_HALO:SHORT_HALO + t, :] = cin[:, BR:2 * BR] * cin[:, 2 * BR:]
    conv = jnp.zeros((t, BR), jnp.float32)
    for kk in range(SHORT_CONV):
        s = SHORT_CONV - 1 - kk
        conv = conv + zbuf[SHORT_HALO - s:SHORT_HALO - s + t, :] * scw_ref[kk:kk + 1, :]
    yc_ref[0] = (cin[:, :BR] * conv * _silu(gc_lin)).astype(yc_ref.dtype)
    zbuf[0:SHORT_HALO, :] = zbuf[t:t + SHORT_HALO, :]

    gb_ref[0] = _silu(gb_lin).astype(gb_ref.dtype)
    vt_ref[0] = vt.astype(vt_ref.dtype)

    lane_head = lax.broadcasted_iota(jnp.int32, (CHUNK, BR), 1) // HEAD_DIM
    ga = _silu(ga_lin)
    for c in range(t // CHUNK):
        sl = slice(c * CHUNK, (c + 1) * CHUNK)
        mixed = sgub_ref[...]
        for h in range(N_HEADS):
            mixed = mixed + jnp.where(lane_head == h, malls[c][h * CHUNK:(h + 1) * CHUNK], 0.0)
        ya_ref[0, sl, :] = (u[sl] * mixed * ga[sl]).astype(ya_ref.dtype)


def _mixer_in(layer, x, g, w_bf, wqk, w_hi, wvt, fb, sguw, sgub, aln, scw, dww, dvec, place):
    bsz, seq, d = x.shape
    t = T_MIX
    nblk = seq // t
    bf = jnp.bfloat16
    const = lambda shape: pl.BlockSpec(shape, lambda b, j: (0,) * len(shape))
    tok = lambda width: pl.BlockSpec((1, t, width), lambda b, j: (b, j, 0))
    return pl.pallas_call(
        _mixer_in_kernel,
        grid=(bsz, nblk),
        in_specs=[
            tok(d), const(g.shape),
            pl.BlockSpec((1, d, W_LO), lambda b, j: (layer, 0, 0)),
            pl.BlockSpec((1, d, 2 * N_HEADS * AUG), lambda b, j: (layer, 0, 0)),
            pl.BlockSpec((1, d, H_BLOCK), lambda b, j: (layer, 0, 0)),
            pl.BlockSpec((1, BR, d), lambda b, j: (layer, 0, 0)), const(fb.shape),
            const(sguw.shape), const(sgub.shape), const(aln.shape), const(scw.shape),
            const(dww.shape), const(dvec.shape), const(place.shape),
        ],
        out_specs=[
            tok(BR), tok(BR), tok(BR), tok(BR),
            pl.BlockSpec((1, N_HEADS, t, AUG), lambda b, j: (b, 0, j, 0)),
            pl.BlockSpec((1, N_HEADS, t, AUG), lambda b, j: (b, 0, j, 0)),
            pl.BlockSpec((1, BR, t), lambda b, j: (b, 0, j)),
        ],
        out_shape=[
            jax.ShapeDtypeStruct((bsz, seq, BR), bf),
            jax.ShapeDtypeStruct((bsz, seq, BR), bf),
            jax.ShapeDtypeStruct((bsz, seq, BR), bf),
            jax.ShapeDtypeStruct((bsz, seq, BR), bf),
            jax.ShapeDtypeStruct((bsz, N_HEADS, seq, AUG), bf),
            jax.ShapeDtypeStruct((bsz, N_HEADS, seq, AUG), bf),
            jax.ShapeDtypeStruct((bsz, BR, seq), bf),
        ],
        scratch_shapes=[
            pltpu.VMEM((SHORT_HALO + t, BR), jnp.float32),
            pltpu.VMEM((CONF_HALO + t, BR), jnp.float32),
            pltpu.VMEM((1, LANES), jnp.float32),
        ],
        compiler_params=pltpu.CompilerParams(
            dimension_semantics=("arbitrary", "arbitrary"), vmem_limit_bytes=VMEM_LIMIT),
        name="mixer_in",
    )(x, g, w_bf, wqk, w_hi, wvt, fb, sguw, sgub, aln, scw, dww, dvec, place)


def _fox_attn_kernel(q_ref, k_ref, vt_ref, gb_ref, o_ref, m_sc, acc_sc, s_buf):
    tq = q_ref.shape[2]
    tk = tq
    qi = pl.program_id(1)
    m_sc[...] = jnp.full(m_sc.shape, NEG, jnp.float32)
    acc_sc[...] = jnp.zeros(acc_sc.shape, jnp.float32)
    ones_rows = jnp.ones((V_ROWS - HEAD_DIM, tk), jnp.bfloat16)

    def scores(h, kt, slot):
        ks = pl.multiple_of(kt * tk, tk)
        k = k_ref[0, h, pl.ds(ks, tk), :]
        s_buf[slot, :, 0:tq] = lax.dot_general(k, q_ref[0, h], _NT,
                                               preferred_element_type=jnp.float32)
        return slot

    def causal(s):
        kpos = lax.broadcasted_iota(jnp.int32, s.shape, 0)
        qpos = lax.broadcasted_iota(jnp.int32, s.shape, 1)
        return jnp.where(kpos <= qpos, s, NEG)

    def accumulate(h, kt, slot):
        s = s_buf[slot, :, 0:tq]
        ks = pl.multiple_of(kt * tk, tk)
        m_old = m_sc[h]
        m_new = jnp.maximum(m_old, jnp.max(s, axis=0, keepdims=True))
        p = jnp.exp2(s - m_new).astype(jnp.bfloat16)
        vt = jnp.concatenate(
            [vt_ref[0, h * HEAD_DIM:(h + 1) * HEAD_DIM, pl.ds(ks, tk)], ones_rows], axis=0)
        acc_sc[h] = jnp.exp2(m_old - m_new) * acc_sc[h] + _dot(vt, p)
        m_sc[h] = m_new

    def step(kt, parity):
        here, there = (0, N_HEADS) if parity == 0 else (N_HEADS, 0)
        todo = [(h, kt, h) for h in range(1, N_HEADS)] + [(0, kt + 1, there)]
        ready = [here]
        for h in range(N_HEADS):
            while todo and len(ready) < 1 + SCORE_LOOKAHEAD:
                ready.append(scores(*todo.pop(0)))
            accumulate(h, kt, ready.pop(0))

    half = tk // 2

    def diag_scores(h, ks, slot0):
        if h == 0:
            return slot0
        k_top = k_ref[0, h, pl.ds(ks, half), :]
        k_bot = k_ref[0, h, pl.ds(pl.multiple_of(ks + half, half), half), :]
        s_buf[h, 0:half, 0:tq] = lax.dot_general(k_top, q_ref[0, h], _NT,
                                                 preferred_element_type=jnp.float32)
        s_buf[h, half:tk, half:tq] = lax.dot_general(k_bot, q_ref[0, h, half:, :], _NT,
                                                     preferred_element_type=jnp.float32)
        return h

    def diag_accumulate(h, ks, slot):
        s_tl = causal(s_buf[slot, 0:half, 0:half])
        s_tr = s_buf[slot, 0:half, half:tq]
        s_br = causal(s_buf[slot, half:tk, half:tq])
        m_old = m_sc[h]
        m_l = jnp.maximum(m_old[:, :half], jnp.max(s_tl, axis=0, keepdims=True))
        m_r = jnp.maximum(
            m_old[:, half:],
            jnp.maximum(jnp.max(s_tr, axis=0, keepdims=True), jnp.max(s_br, axis=0, keepdims=True)))
        p_tl = jnp.exp2(s_tl - m_l).astype(jnp.bfloat16)
        p_tr = jnp.exp2(s_tr - m_r).astype(jnp.bfloat16)
        p_br = jnp.exp2(s_br - m_r).astype(jnp.bfloat16)
        rows = slice(h * HEAD_DIM, (h + 1) * HEAD_DIM)
        vt_top = jnp.concatenate([vt_ref[0, rows, pl.ds(ks, half)], ones_rows[:, :half]], axis=0)
        vt_bot = jnp.concatenate(
            [vt_ref[0, rows, pl.ds(pl.multiple_of(ks + half, half), half)], ones_rows[:, :half]],
            axis=0)
        acc = acc_sc[h]
        acc_l = jnp.exp2(m_old[:, :half] - m_l) * acc[:, :half] + _dot(vt_top, p_tl)
        acc_r = (jnp.exp2(m_old[:, half:] - m_r) * acc[:, half:]
                 + _dot(vt_top, p_tr) + _dot(vt_bot, p_br))
        acc_sc[h] = jnp.concatenate([acc_l, acc_r], axis=1)

    def diag_step(kt, slot0):
        ks = pl.multiple_of(kt * tk, tk)
        todo = list(range(1, N_HEADS))
        ready = [diag_scores(0, ks, slot0)]
        for h in range(N_HEADS):
            while todo and len(ready) < 1 + SCORE_LOOKAHEAD:
                ready.append(diag_scores(todo.pop(0), ks, slot0))
            diag_accumulate(h, ks, ready.pop(0))

    def body2(j, carry):
        step(2 * j, 0)
        step(2 * j + 1, 1)
        return carry

    def finish():
        outs = []
        for h in range(N_HEADS):
            acc = acc_sc[h]
            outs.append(acc[:HEAD_DIM] / acc[HEAD_DIM:HEAD_DIM + 1])
        o = jnp.concatenate(outs, axis=0).T
        o_ref[0] = (o * gb_ref[0].astype(jnp.float32)).astype(o_ref.dtype)

    scores(0, 0, 0)

    @pl.when(qi == 0)
    def _():
        diag_step(qi, 0)
        finish()

    @pl.when(qi % 2 == 1)
    def _():
        lax.fori_loop(0, (qi - 1) // 2, body2, 0)
        step(qi - 1, 0)
        diag_step(qi, N_HEADS)
        finish()

    @pl.when((qi % 2 == 0) & (qi > 0))
    def _():
        lax.fori_loop(0, (qi - 2) // 2, body2, 0)
        step(qi - 2, 0)
        step(qi - 1, 1)
        diag_step(qi, 0)
        finish()


def _fox_attn(q_aug, k_aug, vt, gb):
    bsz, nh, seq, aug = q_aug.shape
    tq = T_ATT
    return pl.pallas_call(
        _fox_attn_kernel,
        grid=(bsz, seq // tq),
        in_specs=[
            pl.BlockSpec((1, nh, tq, aug), lambda b, i: (b, 0, i, 0)),
            pl.BlockSpec((1, nh, seq, aug), lambda b, i: (b, 0, 0, 0)),
            pl.BlockSpec((1, BR, seq), lambda b, i: (b, 0, 0)),
            pl.BlockSpec((1, tq, BR), lambda b, i: (b, i, 0)),
        ],
        out_specs=pl.BlockSpec((1, tq, BR), lambda b, i: (b, i, 0)),
        out_shape=jax.ShapeDtypeStruct((bsz, seq, BR), jnp.bfloat16),
        scratch_shapes=[
            pltpu.VMEM((nh, 1, tq), jnp.float32),
            pltpu.VMEM((nh, V_ROWS, tq), jnp.float32),
            pltpu.VMEM((nh + 1, tq, tq + LANES), jnp.float32),
        ],
        compiler_params=pltpu.CompilerParams(
            dimension_semantics=("arbitrary", "arbitrary"), vmem_limit_bytes=VMEM_LIMIT),
        name="fox_attn",
    )(q_aug, k_aug, vt, gb)


def _merge_out_kernel(x_ref, g_ref, ya_ref, yb_ref, yc_ref, yd_ref,
                      wm0_ref, wm1_ref, wb_ref, wo_ref, fg_ref, o_ref, *, final):
    x = x_ref[0]
    hb = (_rms_scale(x) * g_ref[...]).astype(jnp.bfloat16)
    ys = (ya_ref[0], yb_ref[0], yc_ref[0], yd_ref[0])
    per_ref = H_BLOCK // D_MODEL
    merged = jnp.zeros(x.shape, jnp.float32)
    for n in range(N_BRANCH):
        wm_ref = (wm0_ref, wm1_ref)[n // per_ref]
        c0 = (n % per_ref) * D_MODEL
        gate = _sigmoid(_dot(hb, wm_ref[0, :, c0:c0 + D_MODEL]))
        merged = merged + gate * _dot(ys[n], wb_ref[n])
    out = x + _dot(merged.astype(jnp.bfloat16), wo_ref[...])
    if final:
        out = _rms_scale(out) * fg_ref[...]
    o_ref[0] = out


def _merge_out(layer, x, g, ya, yb, yc, yd, w_hi, wb, wo, fg, final):
    bsz, seq, d = x.shape
    t = T_OUT
    const = lambda shape: pl.BlockSpec(shape, lambda b, j: (0,) * len(shape))
    tok = lambda width: pl.BlockSpec((1, t, width), lambda b, j: (b, j, 0))
    first = H_MERGE // H_BLOCK
    return pl.pallas_call(
        functools.partial(_merge_out_kernel, final=final),
        grid=(bsz, seq // t),
        in_specs=[
            tok(d), const(g.shape), tok(BR), tok(BR), tok(BR), tok(BR),
            pl.BlockSpec((1, d, H_BLOCK), lambda b, j: (layer, 0, first)),
            pl.BlockSpec((1, d, H_BLOCK), lambda b, j: (layer, 0, first + 1)),
            const(wb.shape), const(wo.shape), const(fg.shape),
        ],
        out_specs=tok(d),
        out_shape=jax.ShapeDtypeStruct((bsz, seq, d), jnp.float32),
        compiler_params=pltpu.CompilerParams(
            dimension_semantics=("arbitrary", "arbitrary"), vmem_limit_bytes=VMEM_LIMIT),
        name="merge_out_final" if final else "merge_out",
    )(x, g, ya, yb, yc, yd, w_hi, w_hi, wb, wo, fg)


def _weight_prep_kernel(wa_in, wq_in, wh_in, wa_ref, wqk_ref, wvt_ref, whi_ref):
    bf = jnp.bfloat16
    depth = wh_in.shape[1]
    for l in range(depth):
        whi_ref[l] = wh_in[:, l, :].T.astype(bf)

    @pl.when(pl.program_id(0) == 0)
    def _():
        lane = lax.broadcasted_iota(jnp.int32, (wa_in.shape[2], AUG), 1)
        for l in range(depth):
            wa_ref[l] = wa_in[:, l, :].T.astype(bf)
            xq = wq_in[:, l, :]
            wvt_ref[l] = xq[2 * BR:3 * BR].astype(bf)
            x = xq.T
            f_col = x[:, 3 * BR:3 * BR + LANES]
            for h in range(N_HEADS):
                lo, _ = _head_lanes(h)
                in_head = (lane >= lo) & (lane < lo + HEAD_DIM)
                c0 = (h // 2) * LANES
                q_col = x[:, c0:c0 + LANES] * (LOG2E / math.sqrt(HEAD_DIM))
                k_col = x[:, c0 + BR:c0 + BR + LANES]
                rest = jnp.where(lane < F_LANE + N_HEADS, f_col, 0.0) if h == F_HEAD else 0.0
                wqk_ref[l, :, h * AUG:(h + 1) * AUG] = jnp.where(in_head, q_col, rest).astype(bf)
                wqk_ref[l, :, (N_HEADS + h) * AUG:(N_HEADS + h + 1) * AUG] = (
                    jnp.where(in_head, k_col, 0.0).astype(bf))


def _weight_prep(w_in):
    depth, d, cols = w_in.shape
    bf = jnp.bfloat16
    w_t = jnp.transpose(w_in, (2, 0, 1))
    n_hi = (cols - W_HI) // PREP_COLS
    window = lambda n: (pl.Element(n), pl.Element(depth), pl.Element(d))
    return pl.pallas_call(
        _weight_prep_kernel,
        grid=(n_hi,),
        in_specs=[
            pl.BlockSpec(window(W_LO), lambda i: (0, 0, 0)),
            pl.BlockSpec(window(QKVF_COLS), lambda i: (W_LO, 0, 0)),
            pl.BlockSpec(window(PREP_COLS), lambda i: (W_HI + PREP_COLS * i, 0, 0)),
        ],
        out_specs=[pl.BlockSpec((depth, d, W_LO), lambda i: (0, 0, 0)),
                   pl.BlockSpec((depth, d, 2 * N_HEADS * AUG), lambda i: (0, 0, 0)),
                   pl.BlockSpec((depth, BR, d), lambda i: (0, 0, 0)),
                   pl.BlockSpec((depth, d, PREP_COLS), lambda i: (0, 0, i))],
        out_shape=[jax.ShapeDtypeStruct((depth, d, W_LO), bf),
                   jax.ShapeDtypeStruct((depth, d, 2 * N_HEADS * AUG), bf),
                   jax.ShapeDtypeStruct((depth, BR, d), bf),
                   jax.ShapeDtypeStruct((depth, d, cols - W_HI), bf)],
        compiler_params=pltpu.CompilerParams(
            dimension_semantics=("arbitrary",), vmem_limit_bytes=VMEM_LIMIT),
        name="weight_prep",
    )(w_t, w_t, w_t)


def _bias_placement():
    place = np.zeros((LANES, 2 * N_HEADS * AUG), np.float32)
    for h in range(N_HEADS):
        _, a0 = _head_lanes(h)
        qc, kc = h * AUG + a0, (N_HEADS + h) * AUG + a0
        for piece in range(3):
            src = F_LANE + piece * N_HEADS + h
            place[src, qc + piece] = 1.0
            place[src, kc + 3 + piece] = -1.0
            place[ONES_LANE, qc + 3 + piece] = 1.0
            place[ONES_LANE, kc + piece] = 1.0
    return jnp.asarray(place, jnp.bfloat16)


def kernel(x, norm_g, w_in, f_bias, sgu_w, sgu_b, sgu_ln_g, sgu_ln_b, short_conv_w,
           conf_dw_w, conf_dw_b, conf_ln_g, conf_ln_b, w_branch, w_out, final_g):
    depth = norm_g.shape[0]
    bsz, seq, d_model = x.shape
    assert d_model == D_MODEL and w_in.shape[2] == W_HI + H_MERGE + N_BRANCH * D_MODEL
    assert seq % T_MIX == 0 and seq % T_ATT == 0 and seq % T_OUT == 0
    bf = jnp.bfloat16
    fg = final_g.reshape(1, D_MODEL)
    w_bf, wqk, wvt, w_hi = _weight_prep(w_in)
    wb_bf = w_branch.astype(bf)
    wo_bf = w_out.astype(bf)
    place = _bias_placement()
    for layer in range(depth):
        g = norm_g[layer].reshape(1, D_MODEL)
        fb = jnp.zeros((1, LANES), jnp.float32).at[0, F_LANE:F_LANE + N_HEADS].set(f_bias[layer])
        sguw = sgu_w[layer].reshape(N_HEADS * CHUNK, CHUNK)
        sgub = jnp.repeat(sgu_b[layer].T, HEAD_DIM, axis=1)
        aln = jnp.stack([sgu_ln_g[layer], sgu_ln_b[layer]])
        scw = jnp.zeros((SUBLANES, BR), jnp.float32).at[:SHORT_CONV].set(short_conv_w[layer])
        dww = jnp.broadcast_to(conf_dw_w[layer][:, None, :], (CONF_CONV, SUBLANES, BR))
        dvec = jnp.stack([conf_dw_b[layer], conf_ln_g[layer], conf_ln_b[layer]])
        ya, yc, yd, gb, q_aug, k_aug, vt = _mixer_in(
            layer, x, g, w_bf, wqk, w_hi, wvt, fb, sguw, sgub, aln, scw, dww, dvec, place)
        yb = _fox_attn(q_aug, k_aug, vt, gb)
        x = _merge_out(layer, x, g, ya, yb, yc, yd, w_hi, wb_bf[layer], wo_bf[layer], fg,
                       final=(layer == depth - 1))
    return x
```
